```python
import jax, jax.numpy as jnp
from jax import lax
import numpy as np

D_MODEL = 1024
BATCH = 8
SEQ = 2048
DEPTH = 4

N_MIXERS = 2
MOBA_HEADS = 16
MOBA_HEAD_DIM = D_MODEL // MOBA_HEADS
MOBA_BLOCK = 256
MOBA_TOPK = 3
MOBA_QCHUNK = 128
GMLP_EXPAND = 6
GMLP_DV = GMLP_EXPAND * D_MODEL // 2
GMLP_GROUPS = 8
GMLP_GROUP_DIM = GMLP_DV // GMLP_GROUPS
GMLP_CHUNK = 128
MOE_GROUPS = 8
MOE_EXPERTS_PER_GROUP = 8
MOE_EXPERTS = MOE_GROUPS * MOE_EXPERTS_PER_GROUP
MOE_TOPK = 2
MOE_D_EXPERT = D_MODEL // 4
MOE_ROW_BLOCK = 128
LN_EPS = 1e-5
DEEPNORM_ALPHA = (2.0 * DEPTH) ** 0.25
DEEPNORM_BETA = (8.0 * DEPTH) ** -0.25
N_MOBA_LAYERS = (DEPTH + 1) // 2
N_GMLP_LAYERS = DEPTH // 2

kernel_name = "hybrid_moba_gmlp_hmoe_deepnorm"


def layer_norm(x, g, b):
    xf = x.astype(jnp.float32)
    mu = jnp.mean(xf, axis=-1, keepdims=True)
    var = jnp.mean(jnp.square(xf - mu), axis=-1, keepdims=True)
    y = (xf - mu) * lax.rsqrt(var + LN_EPS)
    return (y * g + b).astype(x.dtype)


def moba_attention(x, w_qkv, w_o):
    B, S, D = x.shape
    H, DH, KB, QC = MOBA_HEADS, MOBA_HEAD_DIM, MOBA_BLOCK, MOBA_QCHUNK
    nb = -(-S // KB)
    nq = S // QC
    k_sel = min(MOBA_TOPK, nb)
    scale = DH ** -0.5

    qkv = (x @ w_qkv).reshape(B, S, 3, H, DH)
    q = qkv[:, :, 0].transpose(0, 2, 1, 3)
    k = qkv[:, :, 1].transpose(0, 2, 1, 3)
    v = qkv[:, :, 2].transpose(0, 2, 1, 3)
    pad = nb * KB - S
    k_blocks = jnp.pad(k, ((0, 0), (0, 0), (0, pad), (0, 0))).reshape(B, H, nb, KB, DH)
    v_blocks = jnp.pad(v, ((0, 0), (0, 0), (0, pad), (0, 0))).reshape(B, H, nb, KB, DH)
    k_mean = jnp.mean(k_blocks.astype(jnp.float32), axis=3)

    q_chunks = q.reshape(B, H, nq, QC, DH).transpose(0, 2, 1, 3, 4).reshape(B * nq, H, QC, DH)
    b_ids = jnp.repeat(jnp.arange(B), nq)
    n_ids = jnp.tile(jnp.arange(nq), B)
    head_ids = jnp.arange(H)[:, None, None]

    def step(args):
        qc, b, n = args
        kb, vb, km = k_blocks[b], v_blocks[b], k_mean[b]
        cur = (n * QC) // KB
        blk_scores = jnp.einsum('hqd,hjd->hqj', qc.astype(jnp.float32), km)
        past = jnp.arange(nb) < cur
        blk_scores = jnp.where(past[None, None, :], blk_scores, -jnp.inf)
        _, sel = lax.top_k(blk_scores, k_sel)
        sel_valid = sel < cur
        kg = kb[head_ids, sel]
        vg = vb[head_ids, sel]
        s_sel = jnp.einsum('hqd,hqjkd->hqjk', qc, kg,
                           preferred_element_type=jnp.float32) * scale
        s_sel = jnp.where(sel_valid[..., None], s_sel, -jnp.inf).reshape(H, QC, k_sel * KB)
        k_own = lax.dynamic_slice_in_dim(kb, cur, 1, axis=1)[:, 0]
        v_own = lax.dynamic_slice_in_dim(vb, cur, 1, axis=1)[:, 0]
        s_own = jnp.einsum('hqd,hkd->hqk', qc, k_own,
                           preferred_element_type=jnp.float32) * scale
        q_pos = n * QC + jnp.arange(QC)
        k_pos = cur * KB + jnp.arange(KB)
        s_own = jnp.where((k_pos[None, :] <= q_pos[:, None])[None], s_own, -jnp.inf)
        p = jax.nn.softmax(jnp.concatenate([s_sel, s_own], axis=-1), axis=-1)
        p_sel = p[..., :k_sel * KB].reshape(H, QC, k_sel, KB).astype(vg.dtype)
        p_own = p[..., k_sel * KB:].astype(v_own.dtype)
        out = (jnp.einsum('hqjk,hqjkd->hqd', p_sel, vg)
               + jnp.einsum('hqk,hkd->hqd', p_own, v_own))
        return out.astype(x.dtype)

    o = lax.map(step, (q_chunks, b_ids, n_ids))
    o = o.reshape(B, nq, H, QC, DH).transpose(0, 1, 3, 2, 4).reshape(B, S, D)
    return o @ w_o


def chunked_gmlp(x, w_in, b_in, ln_v_g, ln_v_b, w_s, b_s, w_out):
    B, S, _ = x.shape
    z = jax.nn.gelu(x @ w_in + b_in, approximate=False)
    u, v = jnp.split(z, 2, axis=-1)
    v = layer_norm(v, ln_v_g, ln_v_b)
    v = v.reshape(B, S // GMLP_CHUNK, GMLP_CHUNK, GMLP_GROUPS, GMLP_GROUP_DIM)
    w_causal = jnp.tril(w_s)
    mixed = jnp.einsum('gts,bcsgk->bctgk', w_causal, v) + b_s.T[:, :, None]
    return (u * mixed.reshape(B, S, GMLP_DV)) @ w_out


def grouped_expert_ffn(xf, expert_id, gates, w1, w3, w2):
    T, D = xf.shape
    E, R = MOE_EXPERTS, MOE_ROW_BLOCK
    A = expert_id.size
    n_blk = -(-A // R) + E
    flat_e = expert_id.reshape(-1)
    order = jnp.argsort(flat_e)
    sorted_e = flat_e[order]
    tok = order // expert_id.shape[1]
    counts = jnp.bincount(flat_e, length=E)
    start = jnp.cumsum(counts) - counts
    padded = ((counts + R - 1) // R) * R
    pad_end = jnp.cumsum(padded)
    pad_start = pad_end - padded
    dest = pad_start[sorted_e] + (jnp.arange(A) - start[sorted_e])
    buf = jnp.zeros((n_blk * R, D), xf.dtype).at[dest].set(xf[tok])
    blk_expert = jnp.clip(jnp.searchsorted(pad_end, jnp.arange(n_blk) * R, side='right'), 0, E - 1)

    def run_block(args):
        rows, e = args
        h = jax.nn.silu(rows @ w1[e]) * (rows @ w3[e])
        return h @ w2[e]

    y_buf = lax.map(run_block, (buf.reshape(n_blk, R, D), blk_expert)).reshape(n_blk * R, D)
    g_sorted = gates.reshape(-1)[order].astype(xf.dtype)
    return jax.ops.segment_sum(y_buf[dest] * g_sorted[:, None], tok, num_segments=T)


def hier_moe(x, w_grp, b_grp, w_rt, b_rt, w1, w3, w2):
    B, S, D = x.shape
    T = B * S
    xf = x.reshape(T, D)
    g_prob = jax.nn.softmax((xf @ w_grp + b_grp).astype(jnp.float32), axis=-1)
    g_p, g_idx = lax.top_k(g_prob, 1)
    e_logits = (xf @ w_rt + b_rt).astype(jnp.float32).reshape(T, MOE_GROUPS, MOE_EXPERTS_PER_GROUP)
    e_logits = jnp.take_along_axis(e_logits, g_idx[:, :, None], axis=1)[:, 0]
    e_top, e_local = lax.top_k(e_logits, MOE_TOPK)
    gates = g_p * jax.nn.softmax(e_top, axis=-1)
    expert_id = g_idx * MOE_EXPERTS_PER_GROUP + e_local
    return grouped_expert_ffn(xf, expert_id, gates, w1, w3, w2).reshape(B, S, D)


def setup_inputs(seed: int = 0) -> dict:
    key = jax.random.key(seed)
    ks = jax.random.split(key, 24)
    D, E, F = D_MODEL, MOE_EXPERTS, MOE_D_EXPERT
    nrm = lambda k, shape, s: jax.random.normal(k, shape, jnp.float32) * s
    return {
        "x": nrm(ks[0], (BATCH, SEQ, D), 1.0),
        "moba_w_qkv": nrm(ks[1], (N_MOBA_LAYERS, D, 3 * D), D ** -0.5),
        "moba_w_o": nrm(ks[2], (N_MOBA_LAYERS, D, D), D ** -0.5 * DEEPNORM_BETA),
        "gmlp_w_in": nrm(ks[3], (N_GMLP_LAYERS, D, 2 * GMLP_DV), D ** -0.5),
        "gmlp_b_in": nrm(ks[4], (N_GMLP_LAYERS, 2 * GMLP_DV), 0.02),
        "gmlp_ln_g": 1.0 + nrm(ks[5], (N_GMLP_LAYERS, GMLP_DV), 0.02),
        "gmlp_ln_b": nrm(ks[6], (N_GMLP_LAYERS, GMLP_DV), 0.02),
        "gmlp_w_s": nrm(ks[7], (N_GMLP_LAYERS, GMLP_GROUPS, GMLP_CHUNK, GMLP_CHUNK), GMLP_CHUNK ** -0.5),
        "gmlp_b_s": 1.0 + nrm(ks[8], (N_GMLP_LAYERS, GMLP_GROUPS, GMLP_CHUNK), 0.02),
        "gmlp_w_out": nrm(ks[9], (N_GMLP_LAYERS, GMLP_DV, D), GMLP_DV ** -0.5 * DEEPNORM_BETA),
        "ln1_g": 1.0 + nrm(ks[10], (DEPTH, D), 0.02),
        "ln1_b": nrm(ks[11], (DEPTH, D), 0.02),
        "ln2_g": 1.0 + nrm(ks[12], (DEPTH, D), 0.02),
        "ln2_b": nrm(ks[13], (DEPTH, D), 0.02),
        "moe_w_grp": nrm(ks[14], (DEPTH, D, MOE_GROUPS), D ** -0.5),
        "moe_b_grp": nrm(ks[15], (DEPTH, MOE_GROUPS), 0.01),
        "moe_w_rt": nrm(ks[16], (DEPTH, D, E), D ** -0.5),
        "moe_b_rt": nrm(ks[17], (DEPTH, E), 0.01),
        "moe_w1": nrm(ks[18], (DEPTH, E, D, F), D ** -0.5),
        "moe_w3": nrm(ks[19], (DEPTH, E, D, F), D ** -0.5),
        "moe_w2": nrm(ks[20], (DEPTH, E, F, D), F ** -0.5 * DEEPNORM_BETA),
    }


def reference(x, moba_w_qkv, moba_w_o, gmlp_w_in, gmlp_b_in, gmlp_ln_g, gmlp_ln_b, gmlp_w_s,
              gmlp_b_s, gmlp_w_out, ln1_g, ln1_b, ln2_g, ln2_b, moe_w_grp, moe_b_grp, moe_w_rt,
              moe_b_rt, moe_w1, moe_w3, moe_w2):
    for i in range(DEPTH):
        j = i // N_MIXERS
        if i % N_MIXERS == 0:
            h = moba_attention(x, moba_w_qkv[j], moba_w_o[j])
        else:
            h = chunked_gmlp(x, gmlp_w_in[j], gmlp_b_in[j], gmlp_ln_g[j], gmlp_ln_b[j],
                             gmlp_w_s[j], gmlp_b_s[j], gmlp_w_out[j])
        x = layer_norm(DEEPNORM_ALPHA * x + h, ln1_g[i], ln1_b[i])
        h = hier_moe(x, moe_w_grp[i], moe_b_grp[i], moe_w_rt[i], moe_b_rt[i],
                     moe_w1[i], moe_w3[i], moe_w2[i])
        x = layer_norm(DEEPNORM_ALPHA * x + h, ln2_g[i], ln2_b[i])
    return x
```

```python
import functools

import jax
import jax.numpy as jnp
from jax import lax
from jax.experimental import pallas as pl
from jax.experimental.pallas import tpu as pltpu

F32 = jnp.float32
BF16 = jnp.bfloat16

D_MODEL = 1024
DEPTH = 4
MOBA_HEADS = 16
MOBA_HEAD_DIM = 64
MOBA_BLOCK = 256
MOBA_TOPK = 3
GMLP_DV = 3072
GMLP_GROUPS = 8
GMLP_GROUP_DIM = GMLP_DV // GMLP_GROUPS
GMLP_CHUNK = 128
MOE_GROUPS = 8
MOE_EXPERTS_PER_GROUP = 8
MOE_EXPERTS = 64
MOE_D_EXPERT = 256
LN_EPS = 1e-5
DEEPNORM_ALPHA = (2.0 * DEPTH) ** 0.25

LANES = 128
ROUTER_GROUP_LANE0 = 0
ROUTER_EXPERT_LANE0 = 8
MOE_ROWS = 256
ROW_TILE = 512
COMBINE_TILE = 256
DISPATCH_CHUNK = 256
VMEM_LIMIT = 56 * 1024 * 1024
NEG_INF = float("-inf")


def _params(sem, vmem=VMEM_LIMIT):
    return pltpu.CompilerParams(dimension_semantics=sem, vmem_limit_bytes=vmem)


def _layer_norm(y, g, b):
    mu = jnp.mean(y, axis=-1, keepdims=True)
    yc = y - mu
    var = jnp.mean(yc * yc, axis=-1, keepdims=True)
    return yc * lax.rsqrt(var + LN_EPS) * g + b


def _dot(a, b):
    return jnp.dot(a, b, preferred_element_type=F32)


def _dot_nt(a, b):
    return lax.dot_general(a, b, (((1,), (1,)), ((), ())), preferred_element_type=F32)


def _split_bf16(x):
    hi = x.astype(BF16)
    lo = (x - hi.astype(F32)).astype(BF16)
    return hi, lo


def _qkv_kernel(x_ref, w_ref, o_ref):
    xb = x_ref[...].astype(BF16)
    for c in range(3):
        acc = _dot(xb, w_ref[:, c * D_MODEL:(c + 1) * D_MODEL])
        if c == 0:
            acc = acc * (MOBA_HEAD_DIM ** -0.5)
        o_ref[:, c * D_MODEL:(c + 1) * D_MODEL] = acc.astype(BF16)


def _qkv_proj(x, w_bf16):
    T = x.shape[0]
    return pl.pallas_call(
        _qkv_kernel,
        grid=(T // ROW_TILE,),
        in_specs=[pl.BlockSpec((ROW_TILE, D_MODEL), lambda i: (i, 0)),
                  pl.BlockSpec((D_MODEL, 3 * D_MODEL), lambda i: (0, 0))],
        out_specs=pl.BlockSpec((ROW_TILE, 3 * D_MODEL), lambda i: (i, 0)),
        out_shape=jax.ShapeDtypeStruct((T, 3 * D_MODEL), BF16),
        compiler_params=_params(("parallel",)),
        name="moba_qkv",
    )(x, w_bf16)


def _moba_kernel(q_ref, k_ref, v_ref, o_ref, *, n_blocks):
    KB = MOBA_BLOCK
    i = pl.program_id(2)
    q = q_ref[...]
    lane = lax.broadcasted_iota(jnp.int32, (KB, LANES), 1)
    blk = lax.broadcasted_iota(jnp.int32, (KB, n_blocks), 1)
    row = lax.broadcasted_iota(jnp.int32, (KB, KB), 0)
    col = lax.broadcasted_iota(jnp.int32, (KB, KB), 1)

    kf = k_ref[...].astype(F32).reshape(n_blocks, KB, LANES)
    kmean = jnp.sum(kf, axis=1) * (1.0 / KB)
    km_hi, km_lo = _split_bf16(kmean)

    outs = []
    for h in range(2):
        head_lanes = (lane < MOBA_HEAD_DIM) if h == 0 else (lane >= MOBA_HEAD_DIM)
        qh = jnp.where(head_lanes, q, jnp.zeros_like(q))
        bs = _dot_nt(qh, km_hi) + _dot_nt(qh, km_lo)
        cnt = jnp.zeros((KB, n_blocks), jnp.int32)
        for jp in range(n_blocks):
            cj = bs[:, jp:jp + 1]
            beats = (cj > bs) | ((cj == bs) & (jp < blk))
            cnt = cnt + jnp.where(beats & (jp < i), 1, 0)
        sel = jnp.where((blk < i) & (cnt < MOBA_TOPK), 1.0, 0.0)

        start = pl.multiple_of(i * KB, KB)
        s = _dot_nt(qh, k_ref[pl.ds(start, KB), :])
        s = jnp.where(col <= row, s, NEG_INF)
        m = jnp.max(s, axis=-1, keepdims=True)
        p = jnp.exp(s - m)
        l = jnp.sum(p, axis=-1, keepdims=True)
        acc = _dot(p.astype(BF16), v_ref[pl.ds(start, KB), :])

        def body(j, carry, qh=qh, sel=sel):
            m, l, acc = carry
            st = pl.multiple_of(j * KB, KB)
            s = _dot_nt(qh, k_ref[pl.ds(st, KB), :])
            sel_j = jnp.sum(jnp.where(blk == j, sel, 0.0), axis=-1, keepdims=True)
            s = jnp.where(sel_j > 0.5, s, NEG_INF)
            m_new = jnp.maximum(m, jnp.max(s, axis=-1, keepdims=True))
            a = jnp.exp(m - m_new)
            p = jnp.exp(s - m_new)
            l = a * l + jnp.sum(p, axis=-1, keepdims=True)
            acc = a * acc + _dot(p.astype(BF16), v_ref[pl.ds(st, KB), :])
            return m_new, l, acc

        m, l, acc = lax.fori_loop(0, i, body, (m, l, acc))
        outs.append(acc / l)
    o_ref[...] = jnp.where(lane < MOBA_HEAD_DIM, outs[0], outs[1]).astype(BF16)


def _moba_attention(qkv, batch, seq):
    T = batch * seq
    nb = seq // MOBA_BLOCK
    n_pairs = MOBA_HEADS // 2
    return pl.pallas_call(
        functools.partial(_moba_kernel, n_blocks=nb),
        grid=(batch, n_pairs, nb),
        in_specs=[pl.BlockSpec((MOBA_BLOCK, LANES), lambda b, hp, i: (b * nb + i, hp)),
                  pl.BlockSpec((seq, LANES), lambda b, hp, i: (b, n_pairs + hp)),
                  pl.BlockSpec((seq, LANES), lambda b, hp, i: (b, 2 * n_pairs + hp))],
        out_specs=pl.BlockSpec((MOBA_BLOCK, LANES), lambda b, hp, i: (b * nb + i, hp)),
        out_shape=jax.ShapeDtypeStruct((T, D_MODEL), BF16),
        compiler_params=_params(("parallel", "parallel", "arbitrary")),
        name="moba_attn",
    )(qkv, qkv, qkv)


def _proj_ln_kernel(a_ref, w_ref, x_ref, g_ref, b_ref, o_ref):
    y = DEEPNORM_ALPHA * x_ref[...] + _dot(a_ref[...], w_ref[...])
    o_ref[...] = _layer_norm(y, g_ref[...], b_ref[...])


def _proj_ln(a, w_bf16, x, g, b):
    T = x.shape[0]
    row = lambda i: (i, 0)
    fixed = lambda i: (0, 0)
    return pl.pallas_call(
        _proj_ln_kernel,
        grid=(T // ROW_TILE,),
        in_specs=[pl.BlockSpec((ROW_TILE, D_MODEL), row),
                  pl.BlockSpec((D_MODEL, D_MODEL), fixed),
                  pl.BlockSpec((ROW_TILE, D_MODEL), row),
                  pl.BlockSpec((1, D_MODEL), fixed),
                  pl.BlockSpec((1, D_MODEL), fixed)],
        out_specs=pl.BlockSpec((ROW_TILE, D_MODEL), row),
        out_shape=jax.ShapeDtypeStruct((T, D_MODEL), F32),
        compiler_params=_params(("parallel",)),
        name="moba_out_ln",
    )(a, w_bf16, x, g.reshape(1, -1), b.reshape(1, -1))


def _gelu(z):
    return 0.5 * z * (1.0 + lax.erf(z * (2.0 ** -0.5)))


def _gmlp_kernel(x_ref, win_ref, bin_ref, lg_ref, lb_ref, ws_ref, bs_ref, wout_ref,
                 g1_ref, b1_ref, o_ref, u_scr, v_scr, gate_scr):
    tm = x_ref.shape[0]
    cw = GMLP_DV // 4
    xb = x_ref[...].astype(BF16)
    for c in range(8):
        z = _dot(xb, win_ref[:, c * cw:(c + 1) * cw]) + bin_ref[:, c * cw:(c + 1) * cw]
        z = _gelu(z)
        if c < 4:
            u_scr[:, c * cw:(c + 1) * cw] = z.astype(BF16)
        else:
            v_scr[:, (c - 4) * cw:(c - 3) * cw] = z
    vn = _layer_norm(v_scr[...], lg_ref[...], lb_ref[...]).astype(BF16)

    row = lax.broadcasted_iota(jnp.int32, (GMLP_CHUNK, GMLP_CHUNK), 0)
    col = lax.broadcasted_iota(jnp.int32, (GMLP_CHUNK, GMLP_CHUNK), 1)
    gd = GMLP_GROUP_DIM
    for g in range(GMLP_GROUPS):
        wc = jnp.where(col <= row, ws_ref[g], 0.0).astype(BF16)
        bias = bs_ref[:, g:g + 1]
        for c in range(tm // GMLP_CHUNK):
            r0 = c * GMLP_CHUNK
            mixed = _dot(wc, vn[r0:r0 + GMLP_CHUNK, g * gd:(g + 1) * gd]) + bias
            u = u_scr[r0:r0 + GMLP_CHUNK, g * gd:(g + 1) * gd].astype(F32)
            gate_scr[r0:r0 + GMLP_CHUNK, g * gd:(g + 1) * gd] = (u * mixed).astype(BF16)
    y = DEEPNORM_ALPHA * x_ref[...] + _dot(gate_scr[...], wout_ref[...])
    o_ref[...] = _layer_norm(y, g1_ref[...], b1_ref[...])


def _gmlp_layer(x, w_in, b_in, ln_g, ln_b, w_s, b_s_t, w_out, g1, b1):
    T = x.shape[0]
    row = lambda i: (i, 0)
    fixed = lambda i: (0, 0)
    once = pl.Buffered(1)
    return pl.pallas_call(
        _gmlp_kernel,
        grid=(T // ROW_TILE,),
        in_specs=[pl.BlockSpec((ROW_TILE, D_MODEL), row),
                  pl.BlockSpec((D_MODEL, 2 * GMLP_DV), fixed, pipeline_mode=once),
                  pl.BlockSpec((1, 2 * GMLP_DV), fixed),
                  pl.BlockSpec((1, GMLP_DV), fixed),
                  pl.BlockSpec((1, GMLP_DV), fixed),
                  pl.BlockSpec((GMLP_GROUPS, GMLP_CHUNK, GMLP_CHUNK), lambda i: (0, 0, 0)),
                  pl.BlockSpec((GMLP_CHUNK, GMLP_GROUPS), fixed),
                  pl.BlockSpec((GMLP_DV, D_MODEL), fixed, pipeline_mode=once),
                  pl.BlockSpec((1, D_MODEL), fixed),
                  pl.BlockSpec((1, D_MODEL), fixed)],
        out_specs=pl.BlockSpec((ROW_TILE, D_MODEL), row),
        out_shape=jax.ShapeDtypeStruct((T, D_MODEL), F32),
        scratch_shapes=[pltpu.VMEM((ROW_TILE, GMLP_DV), BF16),
                        pltpu.VMEM((ROW_TILE, GMLP_DV), F32),
                        pltpu.VMEM((ROW_TILE, GMLP_DV), BF16)],
        compiler_params=_params(("parallel",)),
        name="gmlp",
    )(x, w_in, b_in.reshape(1, -1), ln_g.reshape(1, -1), ln_b.reshape(1, -1), w_s, b_s_t,
      w_out, g1.reshape(1, -1), b1.reshape(1, -1))


def _router_kernel(x_ref, w_ref, b_ref, ids_ref, gate_ref, cnt_ref, carry):
    tm = x_ref.shape[0]

    @pl.when(pl.program_id(0) == 0)
    def _():
        carry[...] = jnp.zeros_like(carry)

    xh, xl = _split_bf16(x_ref[...])
    wh, wl = _split_bf16(w_ref[...])
    logits = _dot(xh, wh) + (_dot(xl, wh) + _dot(xh, wl)) + b_ref[...]
    lane = lax.broadcasted_iota(jnp.int32, (tm, LANES), 1)

    def first_max(vals):
        top = jnp.max(vals, axis=-1, keepdims=True)
        idx = jnp.min(jnp.where(vals == top, lane, LANES), axis=-1, keepdims=True)
        return top, idx

    gl = jnp.where(lane < ROUTER_EXPERT_LANE0, logits, NEG_INF)
    g_top, g_idx = first_max(gl)
    g_p = 1.0 / jnp.sum(jnp.exp(gl - g_top), axis=-1, keepdims=True)
    lo = ROUTER_EXPERT_LANE0 + g_idx * MOE_EXPERTS_PER_GROUP
    el = jnp.where((lane >= lo) & (lane < lo + MOE_EXPERTS_PER_GROUP), logits, NEG_INF)
    e1, i1 = first_max(el)
    e2, i2 = first_max(jnp.where(lane == i1, NEG_INF, el))
    d = jnp.exp(e2 - e1)
    gate1 = g_p / (1.0 + d)
    gate2 = g_p * d / (1.0 + d)

    r = lax.broadcasted_iota(jnp.int32, (tm, tm), 0)
    c = lax.broadcasted_iota(jnp.int32, (tm, tm), 1)
    before = jnp.where(c < r, 1.0, 0.0).astype(BF16)
    oh1 = lane == i1
    oh2 = lane == i2
    oh1f = jnp.where(oh1, 1.0, 0.0)
    oh2f = jnp.where(oh2, 1.0, 0.0)
    c1 = _dot(before, oh1f.astype(BF16))
    c2 = _dot(before, oh2f.astype(BF16))
    tot1 = jnp.sum(oh1f, axis=0, keepdims=True)
    tot2 = jnp.sum(oh2f, axis=0, keepdims=True)
    base = carry[...]
    rank1 = jnp.sum(jnp.where(oh1, base + c1, 0.0), axis=-1, keepdims=True)
    rank2 = jnp.sum(jnp.where(oh2, base + tot1 + c2, 0.0), axis=-1, keepdims=True)
    total = base + tot1 + tot2
    carry[...] = total
    cnt_ref[...] = jnp.broadcast_to(total, cnt_ref.shape)

    ids = jnp.where(lane == 0, i1 - ROUTER_EXPERT_LANE0,
          jnp.where(lane == 1, i2 - ROUTER_EXPERT_LANE0,
          jnp.where(lane == 2, rank1.astype(jnp.int32),
          jnp.where(lane == 3, rank2.astype(jnp.int32), 0))))
    ids_ref[...] = ids
    gate_ref[...] = jnp.where(lane == 0, gate1, jnp.where(lane == 1, gate2, 0.0))


def _router(x, w_pad, b_pad):
    T = x.shape[0]
    row = lambda i: (i, 0)
    fixed = lambda i: (0, 0)
    return pl.pallas_call(
        _router_kernel,
        grid=(T // ROW_TILE,),
        in_specs=[pl.BlockSpec((ROW_TILE, D_MODEL), row),
                  pl.BlockSpec((D_MODEL, LANES), fixed),
                  pl.BlockSpec((1, LANES), fixed)],
        out_specs=[pl.BlockSpec((ROW_TILE, LANES), row),
                   pl.BlockSpec((ROW_TILE, LANES), row),
                   pl.BlockSpec((8, LANES), fixed)],
        out_shape=[jax.ShapeDtypeStruct((T, LANES), jnp.int32),
                   jax.ShapeDtypeStruct((T, LANES), F32),
                   jax.ShapeDtypeStruct((8, LANES), F32)],
        scratch_shapes=[pltpu.VMEM((1, LANES), F32)],
        compiler_params=_params(("arbitrary",)),
        name="moe_router",
    )(x, w_pad, b_pad)


def _row_copy(src_hbm, src_row, dst_ref, dst_row, sem):
    return pltpu.make_async_copy(src_hbm.at[pl.ds(src_row, 1)], dst_ref.at[pl.ds(dst_row, 1)], sem)


def _dispatch_kernel(dest_ref, x_hbm, buf_in_hbm, buf_hbm, sem):
    del buf_in_hbm
    n_tok = x_hbm.shape[0]

    def chunk(ci, _):
        def issue(r, _):
            t = ci * DISPATCH_CHUNK + r
            _row_copy(x_hbm, t, buf_hbm, dest_ref[2 * t], sem).start()
            _row_copy(x_hbm, t, buf_hbm, dest_ref[2 * t + 1], sem).start()
            return 0
        lax.fori_loop(0, DISPATCH_CHUNK, issue, 0)

        def drain(r, _):
            _row_copy(x_hbm, 0, buf_hbm, 0, sem).wait()
            _row_copy(x_hbm, 0, buf_hbm, 0, sem).wait()
            return 0
        lax.fori_loop(0, DISPATCH_CHUNK, drain, 0)
        return 0

    lax.fori_loop(0, n_tok // DISPATCH_CHUNK, chunk, 0)


def _dispatch(dest_flat, x, n_rows):
    buf0 = jnp.zeros((n_rows, D_MODEL), F32)
    return pl.pallas_call(
        _dispatch_kernel,
        grid_spec=pltpu.PrefetchScalarGridSpec(
            num_scalar_prefetch=1,
            grid=(1,),
            in_specs=[pl.BlockSpec(memory_space=pl.ANY), pl.BlockSpec(memory_space=pl.ANY)],
            out_specs=pl.BlockSpec(memory_space=pl.ANY),
            scratch_shapes=[pltpu.SemaphoreType.DMA(())]),
        out_shape=jax.ShapeDtypeStruct((n_rows, D_MODEL), F32),
        input_output_aliases={2: 0},
        compiler_params=pltpu.CompilerParams(dimension_semantics=("arbitrary",),
                                             has_side_effects=True),
        name="moe_dispatch",
    )(dest_flat, x, buf0)


def _ffn_kernel(be_ref, nused_ref, rows_ref, w1_ref, w3_ref, w2_ref, o_ref):
    del be_ref
    i = pl.program_id(0)

    @pl.when(i < nused_ref[0])
    def _():
        rows = rows_ref[...].astype(BF16)
        h1 = _dot(rows, w1_ref[0, 0].astype(BF16))
        h3 = _dot(rows, w3_ref[0, 0].astype(BF16))
        h = (h1 * (1.0 / (1.0 + jnp.exp(-h1))) * h3).astype(BF16)
        o_ref[...] = _dot(h, w2_ref[0, 0].astype(BF16))

    @pl.when(i >= nused_ref[0])
    def _():
        o_ref[...] = jnp.zeros_like(o_ref)


def _expert_ffn(blk_expert, n_used, buf, w1, w3, w2, layer):
    n_rows = buf.shape[0]
    wsel = lambda i, be, nu: (layer, be[i], 0, 0)
    rowblk = lambda i, be, nu: (i, 0)
    return pl.pallas_call(
        _ffn_kernel,
        grid_spec=pltpu.PrefetchScalarGridSpec(
            num_scalar_prefetch=2,
            grid=(n_rows // MOE_ROWS,),
            in_specs=[pl.BlockSpec((MOE_ROWS, D_MODEL), rowblk),
                      pl.BlockSpec((1, 1, D_MODEL, MOE_D_EXPERT), wsel),
                      pl.BlockSpec((1, 1, D_MODEL, MOE_D_EXPERT), wsel),
                      pl.BlockSpec((1, 1, MOE_D_EXPERT, D_MODEL), wsel)],
            out_specs=pl.BlockSpec((MOE_ROWS, D_MODEL), rowblk)),
        out_shape=jax.ShapeDtypeStruct((n_rows, D_MODEL), F32),
        compiler_params=_params(("arbitrary",)),
        name="moe_ffn",
    )(blk_expert, n_used, buf, w1, w3, w2)


def _combine_kernel(dest_ref, y_hbm, x_ref, gate_ref, g_ref, b_ref, o_ref, y_scr, sem):
    tm = x_ref.shape[0]
    t0 = pl.program_id(0) * tm

    def issue(r, _):
        t = t0 + r
        _row_copy(y_hbm, dest_ref[2 * t], y_scr.at[0], r, sem).start()
        _row_copy(y_hbm, dest_ref[2 * t + 1], y_scr.at[1], r, sem).start()
        return 0
    lax.fori_loop(0, tm, issue, 0)

    def drain(r, _):
        _row_copy(y_hbm, 0, y_scr.at[0], 0, sem).wait()
        _row_copy(y_hbm, 0, y_scr.at[1], 0, sem).wait()
        return 0
    lax.fori_loop(0, tm, drain, 0)

    gates = gate_ref[...]
    y = (DEEPNORM_ALPHA * x_ref[...] + gates[:, 0:1] * y_scr[0] + gates[:, 1:2] * y_scr[1])
    o_ref[...] = _layer_norm(y, g_ref[...], b_ref[...])


def _combine_ln(dest_flat, y_buf, x, gates, g, b):
    T = x.shape[0]
    row = lambda i, d: (i, 0)
    fixed = lambda i, d: (0, 0)
    return pl.pallas_call(
        _combine_kernel,
        grid_spec=pltpu.PrefetchScalarGridSpec(
            num_scalar_prefetch=1,
            grid=(T // COMBINE_TILE,),
            in_specs=[pl.BlockSpec(memory_space=pl.ANY),
                      pl.BlockSpec((COMBINE_TILE, D_MODEL), row),
                      pl.BlockSpec((COMBINE_TILE, LANES), row),
                      pl.BlockSpec((1, D_MODEL), fixed),
                      pl.BlockSpec((1, D_MODEL), fixed)],
            out_specs=pl.BlockSpec((COMBINE_TILE, D_MODEL), row),
            scratch_shapes=[pltpu.VMEM((2, COMBINE_TILE, D_MODEL), F32),
                            pltpu.SemaphoreType.DMA(())]),
        out_shape=jax.ShapeDtypeStruct((T, D_MODEL), F32),
        compiler_params=_params(("arbitrary",)),
        name="moe_combine_ln",
    )(dest_flat, y_buf, x, gates, g.reshape(1, -1), b.reshape(1, -1))


def _hier_moe_ln(x, w_grp, b_grp, w_rt, b_rt, w1, w3, w2, layer, g2, b2):
    T = x.shape[0]
    pad_w = LANES - MOE_GROUPS - MOE_EXPERTS
    w_pad = jnp.concatenate([w_grp, w_rt, jnp.zeros((D_MODEL, pad_w), F32)], axis=1)
    b_pad = jnp.concatenate([b_grp, b_rt, jnp.zeros((pad_w,), F32)]).reshape(1, LANES)
    ids, gates, cnt = _router(x, w_pad, b_pad)

    counts = cnt[0, ROUTER_EXPERT_LANE0:ROUTER_EXPERT_LANE0 + MOE_EXPERTS].astype(jnp.int32)
    padded = ((counts + MOE_ROWS - 1) // MOE_ROWS) * MOE_ROWS
    pad_end = jnp.cumsum(padded)
    pad_start = pad_end - padded
    dest = (pad_start[ids[:, 0:2]] + ids[:, 2:4]).reshape(-1)
    n_blk = (2 * T) // MOE_ROWS + MOE_EXPERTS
    blk_expert = jnp.clip(jnp.searchsorted(pad_end, jnp.arange(n_blk) * MOE_ROWS, side='right'),
                          0, MOE_EXPERTS - 1).astype(jnp.int32)
    n_used = (pad_end[-1:] // MOE_ROWS).astype(jnp.int32)

    buf = _dispatch(dest, x, n_blk * MOE_ROWS)
    y_buf = _expert_ffn(blk_expert, n_used, buf, w1, w3, w2, layer)
    return _combine_ln(dest, y_buf, x, gates, g2, b2)


def kernel(x, moba_w_qkv, moba_w_o, gmlp_w_in, gmlp_b_in, gmlp_ln_g, gmlp_ln_b, gmlp_w_s,
           gmlp_b_s, gmlp_w_out, ln1_g, ln1_b, ln2_g, ln2_b, moe_w_grp, moe_b_grp, moe_w_rt,
           moe_b_rt, moe_w1, moe_w3, moe_w2):
    B, S, D = x.shape
    assert D == D_MODEL and S % MOBA_BLOCK == 0 and (B * S) % ROW_TILE == 0
    xf = x.reshape(B * S, D)
    for i in range(DEPTH):
        j = i // 2
        if i % 2 == 0:
            qkv = _qkv_proj(xf, moba_w_qkv[j].astype(BF16))
            att = _moba_attention(qkv, B, S)
            xf = _proj_ln(att, moba_w_o[j].astype(BF16), xf, ln1_g[i], ln1_b[i])
        else:
            xf = _gmlp_layer(xf, gmlp_w_in[j].astype(BF16), gmlp_b_in[j], gmlp_ln_g[j],
                             gmlp_ln_b[j], gmlp_w_s[j], gmlp_b_s[j].T,
                             gmlp_w_out[j].astype(BF16), ln1_g[i], ln1_b[i])
        xf = _hier_moe_ln(xf, moe_w_grp[i], moe_b_grp[i], moe_w_rt[i], moe_b_rt[i],
                          moe_w1, moe_w3, moe_w2, i, ln2_g[i], ln2_b[i])
    return xf.reshape(B, S, D)
```

```python
import functools

import jax
import jax.numpy as jnp
from jax import lax
from jax.experimental import pallas as pl
from jax.experimental.pallas import tpu as pltpu

F32 = jnp.float32
BF16 = jnp.bfloat16

D_MODEL = 1024
DEPTH = 4
MOBA_HEADS = 16
MOBA_HEAD_DIM = 64
MOBA_BLOCK = 256
MOBA_TOPK = 3
GMLP_DV = 3072
GMLP_GROUPS = 8
GMLP_GROUP_DIM = GMLP_DV // GMLP_GROUPS
GMLP_CHUNK = 128
MOE_GROUPS = 8
MOE_EXPERTS_PER_GROUP = 8
MOE_EXPERTS = 64
MOE_D_EXPERT = 256
LN_EPS = 1e-5
DEEPNORM_ALPHA = (2.0 * DEPTH) ** 0.25

LANES = 128
ROUTER_GROUP_LANE0 = 0
ROUTER_EXPERT_LANE0 = 8
MOE_ROWS = 256
ROW_TILE = 512
COMBINE_TILE = 256
ISSUE_UNROLL = 8
VMEM_LIMIT = 56 * 1024 * 1024
NEG_INF = float("-inf")


def _params(sem, vmem=VMEM_LIMIT):
    return pltpu.CompilerParams(dimension_semantics=sem, vmem_limit_bytes=vmem)


def _layer_norm(y, g, b):
    mu = jnp.mean(y, axis=-1, keepdims=True)
    yc = y - mu
    var = jnp.mean(yc * yc, axis=-1, keepdims=True)
    return yc * lax.rsqrt(var + LN_EPS) * g + b


def _dot(a, b):
    return jnp.dot(a, b, preferred_element_type=F32)


def _dot_nt(a, b):
    return lax.dot_general(a, b, (((1,), (1,)), ((), ())), preferred_element_type=F32)


def _split_bf16(x):
    hi = x.astype(BF16)
    lo = (x - hi.astype(F32)).astype(BF16)
    return hi, lo


def _qkv_kernel(x_ref, wqk_ref, wvt_ref, qk_ref, vt_ref):
    xb = x_ref[...].astype(BF16)
    for c in range(2):
        acc = _dot(xb, wqk_ref[:, c * D_MODEL:(c + 1) * D_MODEL])
        if c == 0:
            acc = acc * (MOBA_HEAD_DIM ** -0.5)
        qk_ref[:, c * D_MODEL:(c + 1) * D_MODEL] = acc.astype(BF16)
    vt_ref[...] = _dot_nt(wvt_ref[...], xb).astype(BF16)


def _qkv_proj(x, wqk_bf16, wvt_bf16):
    T = x.shape[0]
    return pl.pallas_call(
        _qkv_kernel,
        grid=(T // ROW_TILE,),
        in_specs=[pl.BlockSpec((ROW_TILE, D_MODEL), lambda i: (i, 0)),
                  pl.BlockSpec((D_MODEL, 2 * D_MODEL), lambda i: (0, 0)),
                  pl.BlockSpec((D_MODEL, D_MODEL), lambda i: (0, 0))],
        out_specs=[pl.BlockSpec((ROW_TILE, 2 * D_MODEL), lambda i: (i, 0)),
                   pl.BlockSpec((D_MODEL, ROW_TILE), lambda i: (0, i))],
        out_shape=[jax.ShapeDtypeStruct((T, 2 * D_MODEL), BF16),
                   jax.ShapeDtypeStruct((D_MODEL, T), BF16)],
        compiler_params=_params(("parallel",)),
        name="moba_qkv",
    )(x, wqk_bf16, wvt_bf16)


def _moba_kernel(q_ref, k_ref, vt_ref, o_ref, s_scr, *, n_blocks):
    KB = MOBA_BLOCK
    S = n_blocks * KB
    q = q_ref[...]
    lane = lax.broadcasted_iota(jnp.int32, (S, LANES), 1)
    blk = lax.broadcasted_iota(jnp.int32, (n_blocks, S), 0)
    cur = jnp.right_shift(lax.broadcasted_iota(jnp.int32, (n_blocks, S), 1),
                          KB.bit_length() - 1)
    key_i = lax.broadcasted_iota(jnp.int32, (KB, KB), 0)
    qry_i = lax.broadcasted_iota(jnp.int32, (KB, KB), 1)
    drow = lax.broadcasted_iota(jnp.int32, (LANES, KB), 0)

    kf = k_ref[...].astype(F32).reshape(n_blocks, KB, LANES)
    kmean = jnp.sum(kf, axis=1) * (1.0 / KB)
    km_hi, km_lo = _split_bf16(kmean)

    qh, bias = [], []
    for h in range(2):
        head_lanes = (lane < MOBA_HEAD_DIM) if h == 0 else (lane >= MOBA_HEAD_DIM)
        qh_h = jnp.where(head_lanes, q, jnp.zeros_like(q))
        bs = _dot_nt(km_hi, qh_h) + _dot_nt(km_lo, qh_h)
        cnt = jnp.zeros((n_blocks, S), jnp.int32)
        for jp in range(n_blocks):
            cj = bs[jp:jp + 1, :]
            beats = (cj > bs) | ((cj == bs) & (jp < blk))
            cnt = cnt + jnp.where(beats & (jp < cur), 1, 0)
        qh.append(qh_h)
        bias.append(jnp.where((blk < cur) & (cnt < MOBA_TOPK), 0.0, NEG_INF))

    for i in range(n_blocks):
        outs = []
        for h in range(2):
            slot = 2 * (i % 2) + h
            qi = qh[h][i * KB:(i + 1) * KB]
            m = None
            for j in range(i + 1):
                s = _dot_nt(k_ref[j * KB:(j + 1) * KB, :], qi)
                if j < i:
                    s = s + bias[h][j:j + 1, i * KB:(i + 1) * KB]
                else:
                    s = jnp.where(key_i <= qry_i, s, NEG_INF)
                s_scr[slot, j * KB:(j + 1) * KB, :] = s
                cm = jnp.max(s, axis=0, keepdims=True)
                m = cm if m is None else jnp.maximum(m, cm)
            l = acc = None
            for j in range(i + 1):
                p = jnp.exp(s_scr[slot, j * KB:(j + 1) * KB, :] - m)
                ps = jnp.sum(p, axis=0, keepdims=True)
                pv = _dot(vt_ref[:, j * KB:(j + 1) * KB], p.astype(BF16))
                l = ps if l is None else l + ps
                acc = pv if acc is None else acc + pv
            outs.append(acc / l)
        ot = jnp.where(drow < MOBA_HEAD_DIM, outs[0], outs[1])
        o_ref[i * KB:(i + 1) * KB, :] = ot.T.astype(BF16)


def _moba_attention(qk, vt, batch, seq):
    T = batch * seq
    nb = seq // MOBA_BLOCK
    n_pairs = MOBA_HEADS // 2
    return pl.pallas_call(
        functools.partial(_moba_kernel, n_blocks=nb),
        grid=(batch, n_pairs),
        in_specs=[pl.BlockSpec((seq, LANES), lambda b, hp: (b, hp)),
                  pl.BlockSpec((seq, LANES), lambda b, hp: (b, n_pairs + hp)),
                  pl.BlockSpec((LANES, seq), lambda b, hp: (hp, b))],
        out_specs=pl.BlockSpec((seq, LANES), lambda b, hp: (b, hp)),
        out_shape=jax.ShapeDtypeStruct((T, D_MODEL), BF16),
        scratch_shapes=[pltpu.VMEM((4, seq, MOBA_BLOCK), F32)],
        compiler_params=_params(("parallel", "parallel")),
        name="moba_attn",
    )(qk, qk, vt)


def _proj_ln_kernel(a_ref, w_ref, x_ref, g_ref, b_ref, o_ref):
    y = DEEPNORM_ALPHA * x_ref[...] + _dot(a_ref[...], w_ref[...])
    o_ref[...] = _layer_norm(y, g_ref[...], b_ref[...])


def _proj_ln(a, w_bf16, x, g, b):
    T = x.shape[0]
    row = lambda i: (i, 0)
    fixed = lambda i: (0, 0)
    return pl.pallas_call(
        _proj_ln_kernel,
        grid=(T // ROW_TILE,),
        in_specs=[pl.BlockSpec((ROW_TILE, D_MODEL), row),
                  pl.BlockSpec((D_MODEL, D_MODEL), fixed),
                  pl.BlockSpec((ROW_TILE, D_MODEL), row),
                  pl.BlockSpec((1, D_MODEL), fixed),
                  pl.BlockSpec((1, D_MODEL), fixed)],
        out_specs=pl.BlockSpec((ROW_TILE, D_MODEL), row),
        out_shape=jax.ShapeDtypeStruct((T, D_MODEL), F32),
        compiler_params=_params(("parallel",)),
        name="moba_out_ln",
    )(a, w_bf16, x, g.reshape(1, -1), b.reshape(1, -1))


def _gelu(z):
    return 0.5 * z * (1.0 + lax.erf(z * (2.0 ** -0.5)))


def _gmlp_kernel(x_ref, win_ref, bin_ref, lg_ref, lb_ref, ws_ref, bs_ref, wout_ref,
                 g1_ref, b1_ref, o_ref, u_scr, v_scr, gate_scr):
    tm = x_ref.shape[0]
    cw = GMLP_DV // 4
    xb = x_ref[...].astype(BF16)
    for c in range(8):
        z = _dot(xb, win_ref[:, c * cw:(c + 1) * cw]) + bin_ref[:, c * cw:(c + 1) * cw]
        z = _gelu(z)
        if c < 4:
            u_scr[:, c * cw:(c + 1) * cw] = z.astype(BF16)
        else:
            v_scr[:, (c - 4) * cw:(c - 3) * cw] = z
    vn = _layer_norm(v_scr[...], lg_ref[...], lb_ref[...]).astype(BF16)

    row = lax.broadcasted_iota(jnp.int32, (GMLP_CHUNK, GMLP_CHUNK), 0)
    col = lax.broadcasted_iota(jnp.int32, (GMLP_CHUNK, GMLP_CHUNK), 1)
    gd = GMLP_GROUP_DIM
    for g in range(GMLP_GROUPS):
        wc = jnp.where(col <= row, ws_ref[g], 0.0).astype(BF16)
        bias = bs_ref[:, g:g + 1]
        for c in range(tm // GMLP_CHUNK):
            r0 = c * GMLP_CHUNK
            mixed = _dot(wc, vn[r0:r0 + GMLP_CHUNK, g * gd:(g + 1) * gd]) + bias
            u = u_scr[r0:r0 + GMLP_CHUNK, g * gd:(g + 1) * gd].astype(F32)
            gate_scr[r0:r0 + GMLP_CHUNK, g * gd:(g + 1) * gd] = (u * mixed).astype(BF16)
    y = DEEPNORM_ALPHA * x_ref[...] + _dot(gate_scr[...], wout_ref[...])
    o_ref[...] = _layer_norm(y, g1_ref[...], b1_ref[...])


def _gmlp_layer(x, w_in, b_in, ln_g, ln_b, w_s, b_s_t, w_out, g1, b1):
    T = x.shape[0]
    row = lambda i: (i, 0)
    fixed = lambda i: (0, 0)
    once = pl.Buffered(1)
    return pl.pallas_call(
        _gmlp_kernel,
        grid=(T // ROW_TILE,),
        in_specs=[pl.BlockSpec((ROW_TILE, D_MODEL), row),
                  pl.BlockSpec((D_MODEL, 2 * GMLP_DV), fixed, pipeline_mode=once),
                  pl.BlockSpec((1, 2 * GMLP_DV), fixed),
                  pl.BlockSpec((1, GMLP_DV), fixed),
                  pl.BlockSpec((1, GMLP_DV), fixed),
                  pl.BlockSpec((GMLP_GROUPS, GMLP_CHUNK, GMLP_CHUNK), lambda i: (0, 0, 0)),
                  pl.BlockSpec((GMLP_CHUNK, GMLP_GROUPS), fixed),
                  pl.BlockSpec((GMLP_DV, D_MODEL), fixed, pipeline_mode=once),
                  pl.BlockSpec((1, D_MODEL), fixed),
                  pl.BlockSpec((1, D_MODEL), fixed)],
        out_specs=pl.BlockSpec((ROW_TILE, D_MODEL), row),
        out_shape=jax.ShapeDtypeStruct((T, D_MODEL), F32),
        scratch_shapes=[pltpu.VMEM((ROW_TILE, GMLP_DV), BF16),
                        pltpu.VMEM((ROW_TILE, GMLP_DV), F32),
                        pltpu.VMEM((ROW_TILE, GMLP_DV), BF16)],
        compiler_params=_params(("parallel",)),
        name="gmlp",
    )(x, w_in, b_in.reshape(1, -1), ln_g.reshape(1, -1), ln_b.reshape(1, -1), w_s, b_s_t,
      w_out, g1.reshape(1, -1), b1.reshape(1, -1))


def _router_kernel(x_ref, w_ref, b_ref, ids_ref, gate_ref, cnt_ref, carry):
    tm = x_ref.shape[0]

    @pl.when(pl.program_id(0) == 0)
    def _():
        carry[...] = jnp.zeros_like(carry)

    xh, xl = _split_bf16(x_ref[...])
    wh, wl = _split_bf16(w_ref[...])
    logits = _dot(xh, wh) + (_dot(xl, wh) + _dot(xh, wl)) + b_ref[...]
    lane = lax.broadcasted_iota(jnp.int32, (tm, LANES), 1)

    def first_max(vals):
        top = jnp.max(vals, axis=-1, keepdims=True)
        idx = jnp.min(jnp.where(vals == top, lane, LANES), axis=-1, keepdims=True)
        return top, idx

    gl = jnp.where(lane < ROUTER_EXPERT_LANE0, logits, NEG_INF)
    g_top, g_idx = first_max(gl)
    g_p = 1.0 / jnp.sum(jnp.exp(gl - g_top), axis=-1, keepdims=True)
    lo = ROUTER_EXPERT_LANE0 + g_idx * MOE_EXPERTS_PER_GROUP
    el = jnp.where((lane >= lo) & (lane < lo + MOE_EXPERTS_PER_GROUP), logits, NEG_INF)
    e1, i1 = first_max(el)
    e2, i2 = first_max(jnp.where(lane == i1, NEG_INF, el))
    d = jnp.exp(e2 - e1)
    gate1 = g_p / (1.0 + d)
    gate2 = g_p * d / (1.0 + d)

    r = lax.broadcasted_iota(jnp.int32, (tm, tm), 0)
    c = lax.broadcasted_iota(jnp.int32, (tm, tm), 1)
    before = jnp.where(c < r, 1.0, 0.0).astype(BF16)
    oh1 = lane == i1
    oh2 = lane == i2
    oh1f = jnp.where(oh1, 1.0, 0.0)
    oh2f = jnp.where(oh2, 1.0, 0.0)
    c1 = _dot(before, oh1f.astype(BF16))
    c2 = _dot(before, oh2f.astype(BF16))
    tot1 = jnp.sum(oh1f, axis=0, keepdims=True)
    tot2 = jnp.sum(oh2f, axis=0, keepdims=True)
    base = carry[...]
    rank1 = jnp.sum(jnp.where(oh1, base + c1, 0.0), axis=-1, keepdims=True)
    rank2 = jnp.sum(jnp.where(oh2, base + tot1 + c2, 0.0), axis=-1, keepdims=True)
    total = base + tot1 + tot2
    carry[...] = total
    cnt_ref[...] = jnp.broadcast_to(total, cnt_ref.shape)

    ids = jnp.where(lane == 0, i1 - ROUTER_EXPERT_LANE0,
          jnp.where(lane == 1, i2 - ROUTER_EXPERT_LANE0,
          jnp.where(lane == 2, rank1.astype(jnp.int32),
          jnp.where(lane == 3, rank2.astype(jnp.int32), 0))))
    ids_ref[...] = ids
    gate_ref[...] = jnp.where(lane == 0, gate1, jnp.where(lane == 1, gate2, 0.0))


def _router(x, w_pad, b_pad):
    T = x.shape[0]
    row = lambda i: (i, 0)
    fixed = lambda i: (0, 0)
    return pl.pallas_call(
        _router_kernel,
        grid=(T // ROW_TILE,),
        in_specs=[pl.BlockSpec((ROW_TILE, D_MODEL), row),
                  pl.BlockSpec((D_MODEL, LANES), fixed),
                  pl.BlockSpec((1, LANES), fixed)],
        out_specs=[pl.BlockSpec((ROW_TILE, LANES), row),
                   pl.BlockSpec((ROW_TILE, LANES), row),
                   pl.BlockSpec((8, LANES), fixed)],
        out_shape=[jax.ShapeDtypeStruct((T, LANES), jnp.int32),
                   jax.ShapeDtypeStruct((T, LANES), F32),
                   jax.ShapeDtypeStruct((8, LANES), F32)],
        scratch_shapes=[pltpu.VMEM((1, LANES), F32)],
        compiler_params=_params(("arbitrary",)),
        name="moe_router",
    )(x, w_pad, b_pad)


def _row_copy(src_ref, src_row, dst_ref, dst_row, sem):
    return pltpu.make_async_copy(src_ref.at[pl.ds(src_row, 1)], dst_ref.at[pl.ds(dst_row, 1)], sem)


def _rows_wait(src_ref, dst_ref, n_rows, sem):
    pltpu.make_async_copy(src_ref.at[pl.ds(0, n_rows)], dst_ref.at[pl.ds(0, n_rows)], sem).wait()


def _dispatch_kernel(dest_ref, x_ref, buf_in_hbm, buf_hbm, sem):
    del buf_in_hbm
    tm = x_ref.shape[0]
    t0 = pl.program_id(0) * tm

    def issue(r, _):
        t = t0 + r
        _row_copy(x_ref, r, buf_hbm, dest_ref[2 * t], sem).start()
        _row_copy(x_ref, r, buf_hbm, dest_ref[2 * t + 1], sem).start()
        return 0
    lax.fori_loop(0, tm, issue, 0, unroll=ISSUE_UNROLL)
    _rows_wait(x_ref, buf_hbm, tm, sem)
    _rows_wait(x_ref, buf_hbm, tm, sem)


def _dispatch(dest_flat, x, n_rows):
    T = x.shape[0]
    buf0 = jnp.zeros((n_rows, D_MODEL), F32)
    return pl.pallas_call(
        _dispatch_kernel,
        grid_spec=pltpu.PrefetchScalarGridSpec(
            num_scalar_prefetch=1,
            grid=(T // COMBINE_TILE,),
            in_specs=[pl.BlockSpec((COMBINE_TILE, D_MODEL), lambda i, d: (i, 0)),
                      pl.BlockSpec(memory_space=pl.ANY)],
            out_specs=pl.BlockSpec(memory_space=pl.ANY),
            scratch_shapes=[pltpu.SemaphoreType.DMA(())]),
        out_shape=jax.ShapeDtypeStruct((n_rows, D_MODEL), F32),
        input_output_aliases={2: 0},
        compiler_params=pltpu.CompilerParams(dimension_semantics=("arbitrary",),
                                             has_side_effects=True),
        name="moe_dispatch",
    )(dest_flat, x, buf0)


def _ffn_kernel(be_ref, nused_ref, rows_ref, w1_ref, w3_ref, w2_ref, o_ref):
    del be_ref
    i = pl.program_id(0)

    @pl.when(i < nused_ref[0])
    def _():
        rows = rows_ref[...].astype(BF16)
        h1 = _dot(rows, w1_ref[0, 0].astype(BF16))
        h3 = _dot(rows, w3_ref[0, 0].astype(BF16))
        h = (h1 * (1.0 / (1.0 + jnp.exp(-h1))) * h3).astype(BF16)
        o_ref[...] = _dot(h, w2_ref[0, 0].astype(BF16))


def _expert_ffn(blk_expert, n_used, buf, w1, w3, w2, layer):
    n_rows = buf.shape[0]
    wsel = lambda i, be, nu: (layer, be[i], 0, 0)
    rowblk = lambda i, be, nu: (jnp.minimum(i, nu[0] - 1), 0)
    return pl.pallas_call(
        _ffn_kernel,
        grid_spec=pltpu.PrefetchScalarGridSpec(
            num_scalar_prefetch=2,
            grid=(n_rows // MOE_ROWS,),
            in_specs=[pl.BlockSpec((MOE_ROWS, D_MODEL), rowblk),
                      pl.BlockSpec((1, 1, D_MODEL, MOE_D_EXPERT), wsel),
                      pl.BlockSpec((1, 1, D_MODEL, MOE_D_EXPERT), wsel),
                      pl.BlockSpec((1, 1, MOE_D_EXPERT, D_MODEL), wsel)],
            out_specs=pl.BlockSpec((MOE_ROWS, D_MODEL), rowblk)),
        out_shape=jax.ShapeDtypeStruct((n_rows, D_MODEL), F32),
        input_output_aliases={2: 0},
        compiler_params=_params(("arbitrary",)),
        name="moe_ffn",
    )(blk_expert, n_used, buf, w1, w3, w2)


def _combine_kernel(dest_ref, y_hbm, x_ref, gate_ref, g_ref, b_ref, o_ref, y_scr, sem):
    tm = x_ref.shape[0]
    i = pl.program_id(0)

    def gather(tile, slot):
        def issue(r, _):
            t = tile * tm + r
            _row_copy(y_hbm, dest_ref[2 * t], y_scr.at[slot, 0], r, sem.at[slot]).start()
            _row_copy(y_hbm, dest_ref[2 * t + 1], y_scr.at[slot, 1], r, sem.at[slot]).start()
            return 0
        lax.fori_loop(0, tm, issue, 0, unroll=ISSUE_UNROLL)

    @pl.when(i == 0)
    def _():
        gather(0, 0)

    @pl.when(i + 1 < pl.num_programs(0))
    def _():
        gather(i + 1, (i + 1) % 2)

    slot = i % 2
    _rows_wait(y_hbm, y_scr.at[slot, 0], tm, sem.at[slot])
    _rows_wait(y_hbm, y_scr.at[slot, 1], tm, sem.at[slot])
    gates = gate_ref[...]
    y = (DEEPNORM_ALPHA * x_ref[...] + gates[:, 0:1] * y_scr[slot, 0]
         + gates[:, 1:2] * y_scr[slot, 1])
    o_ref[...] = _layer_norm(y, g_ref[...], b_ref[...])


def _combine_ln(dest_flat, y_buf, x, gates, g, b):
    T = x.shape[0]
    row = lambda i, d: (i, 0)
    fixed = lambda i, d: (0, 0)
    return pl.pallas_call(
        _combine_kernel,
        grid_spec=pltpu.PrefetchScalarGridSpec(
            num_scalar_prefetch=1,
            grid=(T // COMBINE_TILE,),
            in_specs=[pl.BlockSpec(memory_space=pl.ANY),
                      pl.BlockSpec((COMBINE_TILE, D_MODEL), row),
                      pl.BlockSpec((COMBINE_TILE, LANES), row),
                      pl.BlockSpec((1, D_MODEL), fixed),
                      pl.BlockSpec((1, D_MODEL), fixed)],
            out_specs=pl.BlockSpec((COMBINE_TILE, D_MODEL), row),
            scratch_shapes=[pltpu.VMEM((2, 2, COMBINE_TILE, D_MODEL), F32),
                            pltpu.SemaphoreType.DMA((2,))]),
        out_shape=jax.ShapeDtypeStruct((T, D_MODEL), F32),
        compiler_params=_params(("arbitrary",)),
        name="moe_combine_ln",
    )(dest_flat, y_buf, x, gates, g.reshape(1, -1), b.reshape(1, -1))


def _hier_moe_ln(x, w_grp, b_grp, w_rt, b_rt, w1, w3, w2, layer, g2, b2):
    T = x.shape[0]
    pad_w = LANES - MOE_GROUPS - MOE_EXPERTS
    w_pad = jnp.concatenate([w_grp, w_rt, jnp.zeros((D_MODEL, pad_w), F32)], axis=1)
    b_pad = jnp.concatenate([b_grp, b_rt, jnp.zeros((pad_w,), F32)]).reshape(1, LANES)
    ids, gates, cnt = _router(x, w_pad, b_pad)

    counts = cnt[0, ROUTER_EXPERT_LANE0:ROUTER_EXPERT_LANE0 + MOE_EXPERTS].astype(jnp.int32)
    padded = ((counts + MOE_ROWS - 1) // MOE_ROWS) * MOE_ROWS
    pad_end = jnp.cumsum(padded)
    pad_start = pad_end - padded
    dest = (pad_start[ids[:, 0:2]] + ids[:, 2:4]).reshape(-1)
    n_blk = (2 * T) // MOE_ROWS + MOE_EXPERTS
    n_used = (pad_end[-1:] // MOE_ROWS).astype(jnp.int32)
    blk_row0 = jnp.minimum(jnp.arange(n_blk), n_used - 1) * MOE_ROWS
    blk_expert = jnp.sum(pad_end[None, :] <= blk_row0[:, None], axis=1).astype(jnp.int32)

    buf = _dispatch(dest, x, n_blk * MOE_ROWS)
    y_buf = _expert_ffn(blk_expert, n_used, buf, w1, w3, w2, layer)
    return _combine_ln(dest, y_buf, x, gates, g2, b2)


def kernel(x, moba_w_qkv, moba_w_o, gmlp_w_in, gmlp_b_in, gmlp_ln_g, gmlp_ln_b, gmlp_w_s,
           gmlp_b_s, gmlp_w_out, ln1_g, ln1_b, ln2_g, ln2_b, moe_w_grp, moe_b_grp, moe_w_rt,
           moe_b_rt, moe_w1, moe_w3, moe_w2):
    B, S, D = x.shape
    assert D == D_MODEL and S % MOBA_BLOCK == 0 and (B * S) % ROW_TILE == 0
    xf = x.reshape(B * S, D)
    for i in range(DEPTH):
        j = i // 2
        if i % 2 == 0:
            wqk = moba_w_qkv[j, :, :2 * D].astype(BF16)
            wvt = moba_w_qkv[j, :, 2 * D:].T.astype(BF16)
            qk, vt = _qkv_proj(xf, wqk, wvt)
            att = _moba_attention(qk, vt, B, S)
            xf = _proj_ln(att, moba_w_o[j].astype(BF16), xf, ln1_g[i], ln1_b[i])
        else:
            xf = _gmlp_layer(xf, gmlp_w_in[j].astype(BF16), gmlp_b_in[j], gmlp_ln_g[j],
                             gmlp_ln_b[j], gmlp_w_s[j], gmlp_b_s[j].T,
                             gmlp_w_out[j].astype(BF16), ln1_g[i], ln1_b[i])
        xf = _hier_moe_ln(xf, moe_w_grp[i], moe_b_grp[i], moe_w_rt[i], moe_b_rt[i],
                          moe_w1, moe_w3, moe_w2, i, ln2_g[i], ln2_b[i])
    return xf.reshape(B, S, D)
```

```python
import functools

import jax
import jax.numpy as jnp
from jax import lax
from jax.experimental import pallas as pl
from jax.experimental.pallas import tpu as pltpu

F32 = jnp.float32
BF16 = jnp.bfloat16

D_MODEL = 1024
DEPTH = 4
MOBA_HEADS = 16
MOBA_HEAD_DIM = 64
MOBA_BLOCK = 256
MOBA_TOPK = 3
GMLP_DV = 3072
GMLP_GROUPS = 8
GMLP_GROUP_DIM = GMLP_DV // GMLP_GROUPS
GMLP_CHUNK = 128
MOE_GROUPS = 8
MOE_EXPERTS_PER_GROUP = 8
MOE_EXPERTS = 64
MOE_D_EXPERT = 256
LN_EPS = 1e-5
DEEPNORM_ALPHA = (2.0 * DEPTH) ** 0.25

LANES = 128
ROUTER_GROUP_LANE0 = 0
ROUTER_EXPERT_LANE0 = 8
MOE_ROWS = 512
META_E1, META_E2, META_G1, META_G2 = 0, 1, 2, 3
ROW_TILE = 512
COMBINE_TILE = 256
ISSUE_UNROLL = 8
VMEM_LIMIT = 56 * 1024 * 1024
NEG_INF = float("-inf")


def _params(sem, vmem=VMEM_LIMIT):
    return pltpu.CompilerParams(dimension_semantics=sem, vmem_limit_bytes=vmem)


def _layer_norm(y, g, b):
    mu = jnp.mean(y, axis=-1, keepdims=True)
    yc = y - mu
    var = jnp.mean(yc * yc, axis=-1, keepdims=True)
    return yc * lax.rsqrt(var + LN_EPS) * g + b


def _dot(a, b):
    return jnp.dot(a, b, preferred_element_type=F32)


def _dot_nt(a, b):
    return lax.dot_general(a, b, (((1,), (1,)), ((), ())), preferred_element_type=F32)


def _split_bf16(x):
    hi = x.astype(BF16)
    lo = (x - hi.astype(F32)).astype(BF16)
    return hi, lo


def _qkv_kernel(x_ref, wqk_ref, wvt_ref, qk_ref, vt_ref):
    xb = x_ref[...].astype(BF16)
    for c in range(2):
        acc = _dot(xb, wqk_ref[:, c * D_MODEL:(c + 1) * D_MODEL])
        if c == 0:
            acc = acc * (MOBA_HEAD_DIM ** -0.5)
        qk_ref[:, c * D_MODEL:(c + 1) * D_MODEL] = acc.astype(BF16)
    vt_ref[...] = _dot_nt(wvt_ref[...], xb).astype(BF16)


def _qkv_proj(x, wqk_bf16, wvt_bf16):
    T = x.shape[0]
    return pl.pallas_call(
        _qkv_kernel,
        grid=(T // ROW_TILE,),
        in_specs=[pl.BlockSpec((ROW_TILE, D_MODEL), lambda i: (i, 0)),
                  pl.BlockSpec((D_MODEL, 2 * D_MODEL), lambda i: (0, 0)),
                  pl.BlockSpec((D_MODEL, D_MODEL), lambda i: (0, 0))],
        out_specs=[pl.BlockSpec((ROW_TILE, 2 * D_MODEL), lambda i: (i, 0)),
                   pl.BlockSpec((D_MODEL, ROW_TILE), lambda i: (0, i))],
        out_shape=[jax.ShapeDtypeStruct((T, 2 * D_MODEL), BF16),
                   jax.ShapeDtypeStruct((D_MODEL, T), BF16)],
        compiler_params=_params(("parallel",)),
        name="moba_qkv",
    )(x, wqk_bf16, wvt_bf16)


def _moba_kernel(q_ref, k_ref, vt_ref, o_ref, s_scr, *, n_blocks):
    KB = MOBA_BLOCK
    S = n_blocks * KB
    q = q_ref[...]
    lane = lax.broadcasted_iota(jnp.int32, (S, LANES), 1)
    blk = lax.broadcasted_iota(jnp.int32, (n_blocks, S), 0)
    cur = jnp.right_shift(lax.broadcasted_iota(jnp.int32, (n_blocks, S), 1),
                          KB.bit_length() - 1)
    key_i = lax.broadcasted_iota(jnp.int32, (KB, KB), 0)
    qry_i = lax.broadcasted_iota(jnp.int32, (KB, KB), 1)
    drow = lax.broadcasted_iota(jnp.int32, (LANES, KB), 0)

    kf = k_ref[...].astype(F32).reshape(n_blocks, KB, LANES)
    kmean = jnp.sum(kf, axis=1) * (1.0 / KB)
    km_hi, km_lo = _split_bf16(kmean)

    qh, bias = [], []
    for h in range(2):
        head_lanes = (lane < MOBA_HEAD_DIM) if h == 0 else (lane >= MOBA_HEAD_DIM)
        qh_h = jnp.where(head_lanes, q, jnp.zeros_like(q))
        bs = _dot_nt(km_hi, qh_h) + _dot_nt(km_lo, qh_h)
        cnt = jnp.zeros((n_blocks, S), jnp.int32)
        for jp in range(n_blocks):
            cj = bs[jp:jp + 1, :]
            beats = (cj > bs) | ((cj == bs) & (jp < blk))
            cnt = cnt + jnp.where(beats & (jp < cur), 1, 0)
        qh.append(qh_h)
        bias.append(jnp.where((blk < cur) & (cnt < MOBA_TOPK), 0.0, NEG_INF))

    for i in range(n_blocks):
        outs = []
        for h in range(2):
            slot = 2 * (i % 2) + h
            qi = qh[h][i * KB:(i + 1) * KB]
            m = None
            for j in range(i + 1):
                s = _dot_nt(k_ref[j * KB:(j + 1) * KB, :], qi)
                if j < i:
                    s = s + bias[h][j:j + 1, i * KB:(i + 1) * KB]
                else:
                    s = jnp.where(key_i <= qry_i, s, NEG_INF)
                s_scr[slot, j * KB:(j + 1) * KB, :] = s
                cm = jnp.max(s, axis=0, keepdims=True)
                m = cm if m is None else jnp.maximum(m, cm)
            l = acc = None
            for j in range(i + 1):
                p = jnp.exp(s_scr[slot, j * KB:(j + 1) * KB, :] - m)
                ps = jnp.sum(p, axis=0, keepdims=True)
                pv = _dot(vt_ref[:, j * KB:(j + 1) * KB], p.astype(BF16))
                l = ps if l is None else l + ps
                acc = pv if acc is None else acc + pv
            outs.append(acc / l)
        ot = jnp.where(drow < MOBA_HEAD_DIM, outs[0], outs[1])
        o_ref[i * KB:(i + 1) * KB, :] = ot.T.astype(BF16)


def _moba_attention(qk, vt, batch, seq):
    T = batch * seq
    nb = seq // MOBA_BLOCK
    n_pairs = MOBA_HEADS // 2
    return pl.pallas_call(
        functools.partial(_moba_kernel, n_blocks=nb),
        grid=(batch, n_pairs),
        in_specs=[pl.BlockSpec((seq, LANES), lambda b, hp: (b, hp)),
                  pl.BlockSpec((seq, LANES), lambda b, hp: (b, n_pairs + hp)),
                  pl.BlockSpec((LANES, seq), lambda b, hp: (hp, b))],
        out_specs=pl.BlockSpec((seq, LANES), lambda b, hp: (b, hp)),
        out_shape=jax.ShapeDtypeStruct((T, D_MODEL), BF16),
        scratch_shapes=[pltpu.VMEM((4, seq, MOBA_BLOCK), F32)],
        compiler_params=_params(("parallel", "parallel")),
        name="moba_attn",
    )(qk, qk, vt)


def _proj_ln_kernel(a_ref, w_ref, x_ref, g_ref, b_ref, o_ref):
    y = DEEPNORM_ALPHA * x_ref[...] + _dot(a_ref[...], w_ref[...])
    o_ref[...] = _layer_norm(y, g_ref[...], b_ref[...])


def _proj_ln(a, w_bf16, x, g, b):
    T = x.shape[0]
    row = lambda i: (i, 0)
    fixed = lambda i: (0, 0)
    return pl.pallas_call(
        _proj_ln_kernel,
        grid=(T // ROW_TILE,),
        in_specs=[pl.BlockSpec((ROW_TILE, D_MODEL), row),
                  pl.BlockSpec((D_MODEL, D_MODEL), fixed),
                  pl.BlockSpec((ROW_TILE, D_MODEL), row),
                  pl.BlockSpec((1, D_MODEL), fixed),
                  pl.BlockSpec((1, D_MODEL), fixed)],
        out_specs=pl.BlockSpec((ROW_TILE, D_MODEL), row),
        out_shape=jax.ShapeDtypeStruct((T, D_MODEL), F32),
        compiler_params=_params(("parallel",)),
        name="moba_out_ln",
    )(a, w_bf16, x, g.reshape(1, -1), b.reshape(1, -1))


def _gelu(z):
    return 0.5 * z * (1.0 + lax.erf(z * (2.0 ** -0.5)))


def _gmlp_kernel(x_ref, win_ref, bin_ref, lg_ref, lb_ref, ws_ref, bs_ref, wout_ref,
                 g1_ref, b1_ref, o_ref, u_scr, v_scr, gate_scr):
    tm = x_ref.shape[0]
    cw = GMLP_DV // 4
    xb = x_ref[...].astype(BF16)
    for c in range(8):
        z = _dot(xb, win_ref[:, c * cw:(c + 1) * cw]) + bin_ref[:, c * cw:(c + 1) * cw]
        z = _gelu(z)
        if c < 4:
            u_scr[:, c * cw:(c + 1) * cw] = z.astype(BF16)
        else:
            v_scr[:, (c - 4) * cw:(c - 3) * cw] = z
    vn = _layer_norm(v_scr[...], lg_ref[...], lb_ref[...]).astype(BF16)

    row = lax.broadcasted_iota(jnp.int32, (GMLP_CHUNK, GMLP_CHUNK), 0)
    col = lax.broadcasted_iota(jnp.int32, (GMLP_CHUNK, GMLP_CHUNK), 1)
    gd = GMLP_GROUP_DIM
    for g in range(GMLP_GROUPS):
        wc = jnp.where(col <= row, ws_ref[g], 0.0).astype(BF16)
        bias = bs_ref[:, g:g + 1]
        for c in range(tm // GMLP_CHUNK):
            r0 = c * GMLP_CHUNK
            mixed = _dot(wc, vn[r0:r0 + GMLP_CHUNK, g * gd:(g + 1) * gd]) + bias
            u = u_scr[r0:r0 + GMLP_CHUNK, g * gd:(g + 1) * gd].astype(F32)
            gate_scr[r0:r0 + GMLP_CHUNK, g * gd:(g + 1) * gd] = (u * mixed).astype(BF16)
    y = DEEPNORM_ALPHA * x_ref[...] + _dot(gate_scr[...], wout_ref[...])
    o_ref[...] = _layer_norm(y, g1_ref[...], b1_ref[...])


def _gmlp_layer(x, w_in, b_in, ln_g, ln_b, w_s, b_s_t, w_out, g1, b1):
    T = x.shape[0]
    row = lambda i: (i, 0)
    fixed = lambda i: (0, 0)
    once = pl.Buffered(1)
    return pl.pallas_call(
        _gmlp_kernel,
        grid=(T // ROW_TILE,),
        in_specs=[pl.BlockSpec((ROW_TILE, D_MODEL), row),
                  pl.BlockSpec((D_MODEL, 2 * GMLP_DV), fixed, pipeline_mode=once),
                  pl.BlockSpec((1, 2 * GMLP_DV), fixed),
                  pl.BlockSpec((1, GMLP_DV), fixed),
                  pl.BlockSpec((1, GMLP_DV), fixed),
                  pl.BlockSpec((GMLP_GROUPS, GMLP_CHUNK, GMLP_CHUNK), lambda i: (0, 0, 0)),
                  pl.BlockSpec((GMLP_CHUNK, GMLP_GROUPS), fixed),
                  pl.BlockSpec((GMLP_DV, D_MODEL), fixed, pipeline_mode=once),
                  pl.BlockSpec((1, D_MODEL), fixed),
                  pl.BlockSpec((1, D_MODEL), fixed)],
        out_specs=pl.BlockSpec((ROW_TILE, D_MODEL), row),
        out_shape=jax.ShapeDtypeStruct((T, D_MODEL), F32),
        scratch_shapes=[pltpu.VMEM((ROW_TILE, GMLP_DV), BF16),
                        pltpu.VMEM((ROW_TILE, GMLP_DV), F32),
                        pltpu.VMEM((ROW_TILE, GMLP_DV), BF16)],
        compiler_params=_params(("parallel",)),
        name="gmlp",
    )(x, w_in, b_in.reshape(1, -1), ln_g.reshape(1, -1), ln_b.reshape(1, -1), w_s, b_s_t,
      w_out, g1.reshape(1, -1), b1.reshape(1, -1))


def _router_kernel(x_ref, w_ref, b_ref, ids_ref, meta_ref, cnt_ref, carry):
    tm = x_ref.shape[0]

    @pl.when(pl.program_id(0) == 0)
    def _():
        carry[...] = jnp.zeros_like(carry)

    xh, xl = _split_bf16(x_ref[...])
    wh, wl = _split_bf16(w_ref[...])
    logits = _dot(xh, wh) + (_dot(xl, wh) + _dot(xh, wl)) + b_ref[...]
    lane = lax.broadcasted_iota(jnp.int32, (tm, LANES), 1)

    def first_max(vals):
        top = jnp.max(vals, axis=-1, keepdims=True)
        idx = jnp.min(jnp.where(vals == top, lane, LANES), axis=-1, keepdims=True)
        return top, idx

    gl = jnp.where(lane < ROUTER_EXPERT_LANE0, logits, NEG_INF)
    g_top, g_idx = first_max(gl)
    g_p = 1.0 / jnp.sum(jnp.exp(gl - g_top), axis=-1, keepdims=True)
    lo = ROUTER_EXPERT_LANE0 + g_idx * MOE_EXPERTS_PER_GROUP
    el = jnp.where((lane >= lo) & (lane < lo + MOE_EXPERTS_PER_GROUP), logits, NEG_INF)
    e1, i1 = first_max(el)
    e2, i2 = first_max(jnp.where(lane == i1, NEG_INF, el))
    d = jnp.exp(e2 - e1)
    gate1 = g_p / (1.0 + d)
    gate2 = g_p * d / (1.0 + d)

    r = lax.broadcasted_iota(jnp.int32, (tm, tm), 0)
    c = lax.broadcasted_iota(jnp.int32, (tm, tm), 1)
    before = jnp.where(c < r, 1.0, 0.0).astype(BF16)
    oh = lane == g_idx
    ohf = jnp.where(oh, 1.0, 0.0)
    ahead = _dot(before, ohf.astype(BF16))
    base = carry[...]
    rank = jnp.sum(jnp.where(oh, base + ahead, 0.0), axis=-1, keepdims=True)
    total = base + jnp.sum(ohf, axis=0, keepdims=True)
    carry[...] = total
    cnt_ref[...] = jnp.broadcast_to(total, cnt_ref.shape)

    ids_ref[...] = jnp.where(lane == 0, g_idx, jnp.where(lane == 1, rank.astype(jnp.int32), 0))
    meta_ref[...] = jnp.where(lane == META_E1, (i1 - lo).astype(F32),
                    jnp.where(lane == META_E2, (i2 - lo).astype(F32),
                    jnp.where(lane == META_G1, gate1,
                    jnp.where(lane == META_G2, gate2, 0.0))))


def _router(x, w_pad, b_pad):
    T = x.shape[0]
    row = lambda i: (i, 0)
    fixed = lambda i: (0, 0)
    return pl.pallas_call(
        _router_kernel,
        grid=(T // ROW_TILE,),
        in_specs=[pl.BlockSpec((ROW_TILE, D_MODEL), row),
                  pl.BlockSpec((D_MODEL, LANES), fixed),
                  pl.BlockSpec((1, LANES), fixed)],
        out_specs=[pl.BlockSpec((ROW_TILE, LANES), row),
                   pl.BlockSpec((ROW_TILE, LANES), row),
                   pl.BlockSpec((8, LANES), fixed)],
        out_shape=[jax.ShapeDtypeStruct((T, LANES), jnp.int32),
                   jax.ShapeDtypeStruct((T, LANES), F32),
                   jax.ShapeDtypeStruct((8, LANES), F32)],
        scratch_shapes=[pltpu.VMEM((1, LANES), F32)],
        compiler_params=_params(("arbitrary",)),
        name="moe_router",
    )(x, w_pad, b_pad)


def _row_copy(src_ref, src_row, dst_ref, dst_row, sem):
    return pltpu.make_async_copy(src_ref.at[pl.ds(src_row, 1)], dst_ref.at[pl.ds(dst_row, 1)], sem)


def _rows_wait(src_ref, dst_ref, n_rows, sem):
    pltpu.make_async_copy(src_ref.at[pl.ds(0, n_rows)], dst_ref.at[pl.ds(0, n_rows)], sem).wait()


def _dispatch_kernel(dest_ref, x_ref, meta_ref, buf_in_hbm, buf_hbm, row_scr, sem):
    del buf_in_hbm
    tm = x_ref.shape[0]
    i = pl.program_id(0)
    n = pl.num_programs(0)
    slot = i % 2

    @pl.when(i >= 2)
    def _():
        _rows_wait(row_scr.at[slot], buf_hbm, tm, sem.at[slot])

    row_scr[slot, :, :D_MODEL] = x_ref[...]
    row_scr[slot, :, D_MODEL:] = meta_ref[...]

    def issue(r, _):
        _row_copy(row_scr.at[slot], r, buf_hbm, dest_ref[i * tm + r], sem.at[slot]).start()
        return 0
    lax.fori_loop(0, tm, issue, 0, unroll=ISSUE_UNROLL)

    @pl.when(i == n - 1)
    def _():
        _rows_wait(row_scr.at[slot], buf_hbm, tm, sem.at[slot])

        @pl.when(n >= 2)
        def _():
            _rows_wait(row_scr.at[1 - slot], buf_hbm, tm, sem.at[1 - slot])


def _dispatch(dest, x, meta, n_rows):
    T = x.shape[0]
    width = D_MODEL + LANES
    buf0 = jnp.zeros((n_rows, width), F32)
    return pl.pallas_call(
        _dispatch_kernel,
        grid_spec=pltpu.PrefetchScalarGridSpec(
            num_scalar_prefetch=1,
            grid=(T // COMBINE_TILE,),
            in_specs=[pl.BlockSpec((COMBINE_TILE, D_MODEL), lambda i, d: (i, 0)),
                      pl.BlockSpec((COMBINE_TILE, LANES), lambda i, d: (i, 0)),
                      pl.BlockSpec(memory_space=pl.ANY)],
            out_specs=pl.BlockSpec(memory_space=pl.ANY),
            scratch_shapes=[pltpu.VMEM((2, COMBINE_TILE, width), F32),
                            pltpu.SemaphoreType.DMA((2,))]),
        out_shape=jax.ShapeDtypeStruct((n_rows, width), F32),
        input_output_aliases={3: 0},
        compiler_params=pltpu.CompilerParams(dimension_semantics=("arbitrary",),
                                             has_side_effects=True),
        name="moe_dispatch",
    )(dest, x, meta, buf0)


def _ffn_kernel(bg_ref, nused_ref, rows_ref, w1_ref, w3_ref, w2_ref, o_ref):
    del bg_ref
    i = pl.program_id(0)

    @pl.when(i < nused_ref[0])
    def _():
        x = rows_ref[:, :D_MODEL].astype(BF16)
        meta = rows_ref[:, D_MODEL:]
        e1 = meta[:, META_E1:META_E1 + 1]
        e2 = meta[:, META_E2:META_E2 + 1]
        g1 = meta[:, META_G1:META_G1 + 1]
        g2 = meta[:, META_G2:META_G2 + 1]
        acc = None
        for e in range(MOE_EXPERTS_PER_GROUP):
            h1 = _dot(x, w1_ref[0, e].astype(BF16))
            h3 = _dot(x, w3_ref[0, e].astype(BF16))
            gate = jnp.where(e1 == e, g1, g2)
            h = h1 * (1.0 / (1.0 + jnp.exp(-h1))) * h3 * gate
            h = jnp.where((e1 == e) | (e2 == e), h, 0.0).astype(BF16)
            y = _dot(h, w2_ref[0, e].astype(BF16))
            acc = y if acc is None else acc + y
        o_ref[...] = acc

    @pl.when(i >= nused_ref[0])
    def _():
        o_ref[...] = jnp.zeros_like(o_ref)


def _expert_ffn(blk_group, n_used, buf, w1, w3, w2, layer):
    n_rows = buf.shape[0]
    epg = MOE_EXPERTS_PER_GROUP
    wsel = lambda i, bg, nu: (layer, bg[i], 0, 0)
    once = pl.Buffered(1)
    return pl.pallas_call(
        _ffn_kernel,
        grid_spec=pltpu.PrefetchScalarGridSpec(
            num_scalar_prefetch=2,
            grid=(n_rows // MOE_ROWS,),
            in_specs=[pl.BlockSpec((MOE_ROWS, D_MODEL + LANES),
                                   lambda i, bg, nu: (jnp.minimum(i, nu[0] - 1), 0)),
                      pl.BlockSpec((1, epg, D_MODEL, MOE_D_EXPERT), wsel, pipeline_mode=once),
                      pl.BlockSpec((1, epg, D_MODEL, MOE_D_EXPERT), wsel, pipeline_mode=once),
                      pl.BlockSpec((1, epg, MOE_D_EXPERT, D_MODEL), wsel, pipeline_mode=once)],
            out_specs=pl.BlockSpec((MOE_ROWS, D_MODEL), lambda i, bg, nu: (i, 0))),
        out_shape=jax.ShapeDtypeStruct((n_rows, D_MODEL), F32),
        compiler_params=_params(("arbitrary",)),
        name="moe_ffn",
    )(blk_group, n_used, buf, w1, w3, w2)


def _combine_kernel(dest_ref, y_hbm, x_ref, g_ref, b_ref, o_ref, y_scr, sem):
    tm = x_ref.shape[0]
    i = pl.program_id(0)

    def gather(tile, slot):
        def issue(r, _):
            _row_copy(y_hbm, dest_ref[tile * tm + r], y_scr.at[slot], r, sem.at[slot]).start()
            return 0
        lax.fori_loop(0, tm, issue, 0, unroll=ISSUE_UNROLL)

    @pl.when(i == 0)
    def _():
        gather(0, 0)

    @pl.when(i + 1 < pl.num_programs(0))
    def _():
        gather(i + 1, (i + 1) % 2)

    slot = i % 2
    _rows_wait(y_hbm, y_scr.at[slot], tm, sem.at[slot])
    y = DEEPNORM_ALPHA * x_ref[...] + y_scr[slot]
    o_ref[...] = _layer_norm(y, g_ref[...], b_ref[...])


def _combine_ln(dest, y_buf, x, g, b):
    T = x.shape[0]
    row = lambda i, d: (i, 0)
    fixed = lambda i, d: (0, 0)
    return pl.pallas_call(
        _combine_kernel,
        grid_spec=pltpu.PrefetchScalarGridSpec(
            num_scalar_prefetch=1,
            grid=(T // COMBINE_TILE,),
            in_specs=[pl.BlockSpec(memory_space=pl.ANY),
                      pl.BlockSpec((COMBINE_TILE, D_MODEL), row),
                      pl.BlockSpec((1, D_MODEL), fixed),
                      pl.BlockSpec((1, D_MODEL), fixed)],
            out_specs=pl.BlockSpec((COMBINE_TILE, D_MODEL), row),
            scratch_shapes=[pltpu.VMEM((2, COMBINE_TILE, D_MODEL), F32),
                            pltpu.SemaphoreType.DMA((2,))]),
        out_shape=jax.ShapeDtypeStruct((T, D_MODEL), F32),
        compiler_params=_params(("arbitrary",)),
        name="moe_combine_ln",
    )(dest, y_buf, x, g.reshape(1, -1), b.reshape(1, -1))


def _hier_moe_ln(x, w_grp, b_grp, w_rt, b_rt, w1, w3, w2, layer, g2, b2):
    T = x.shape[0]
    pad_w = LANES - MOE_GROUPS - MOE_EXPERTS
    w_pad = jnp.concatenate([w_grp, w_rt, jnp.zeros((D_MODEL, pad_w), F32)], axis=1)
    b_pad = jnp.concatenate([b_grp, b_rt, jnp.zeros((pad_w,), F32)]).reshape(1, LANES)
    ids, meta, cnt = _router(x, w_pad, b_pad)

    counts = cnt[0, ROUTER_GROUP_LANE0:ROUTER_GROUP_LANE0 + MOE_GROUPS].astype(jnp.int32)
    padded = ((counts + MOE_ROWS - 1) // MOE_ROWS) * MOE_ROWS
    pad_end = jnp.cumsum(padded)
    pad_start = pad_end - padded
    dest = pad_start[ids[:, 0]] + ids[:, 1]
    n_blk = T // MOE_ROWS + MOE_GROUPS
    n_used = (pad_end[-1:] // MOE_ROWS).astype(jnp.int32)
    blk_row0 = jnp.minimum(jnp.arange(n_blk), n_used - 1) * MOE_ROWS
    blk_group = jnp.sum(pad_end[None, :] <= blk_row0[:, None], axis=1).astype(jnp.int32)

    buf = _dispatch(dest, x, meta, n_blk * MOE_ROWS)
    y_buf = _expert_ffn(blk_group, n_used, buf, w1, w3, w2, layer)
    return _combine_ln(dest, y_buf, x, g2, b2)


def kernel(x, moba_w_qkv, moba_w_o, gmlp_w_in, gmlp_b_in, gmlp_ln_g, gmlp_ln_b, gmlp_w_s,
           gmlp_b_s, gmlp_w_out, ln1_g, ln1_b, ln2_g, ln2_b, moe_w_grp, moe_b_grp, moe_w_rt,
           moe_b_rt, moe_w1, moe_w3, moe_w2):
    B, S, D = x.shape
    assert D == D_MODEL and S % MOBA_BLOCK == 0 and (B * S) % ROW_TILE == 0
    xf = x.reshape(B * S, D)
    for i in range(DEPTH):
        j = i // 2
        if i % 2 == 0:
            wqk = moba_w_qkv[j, :, :2 * D].astype(BF16)
            wvt = moba_w_qkv[j, :, 2 * D:].T.astype(BF16)
            qk, vt = _qkv_proj(xf, wqk, wvt)
            att = _moba_attention(qk, vt, B, S)
            xf = _proj_ln(att, moba_w_o[j].astype(BF16), xf, ln1_g[i], ln1_b[i])
        else:
            xf = _gmlp_layer(xf, gmlp_w_in[j].astype(BF16), gmlp_b_in[j], gmlp_ln_g[j],
                             gmlp_ln_b[j], gmlp_w_s[j], gmlp_b_s[j].T,
                             gmlp_w_out[j].astype(BF16), ln1_g[i], ln1_b[i])
        xf = _hier_moe_ln(xf, moe_w_grp[i], moe_b_grp[i], moe_w_rt[i], moe_b_rt[i],
                          moe_w1, moe_w3, moe_w2, i, ln2_g[i], ln2_b[i])
    return xf.reshape(B, S, D)
```

```python
import functools

import jax
import jax.numpy as jnp
from jax import lax
from jax.experimental import pallas as pl
from jax.experimental.pallas import tpu as pltpu

F32 = jnp.float32
BF16 = jnp.bfloat16

D_MODEL = 1024
DEPTH = 4
MOBA_HEADS = 16
MOBA_HEAD_DIM = 64
MOBA_BLOCK = 256
MOBA_TOPK = 3
GMLP_DV = 3072
GMLP_GROUPS = 8
GMLP_GROUP_DIM = GMLP_DV // GMLP_GROUPS
GMLP_CHUNK = 128
MOE_GROUPS = 8
MOE_EXPERTS_PER_GROUP = 8
MOE_EXPERTS = 64
MOE_D_EXPERT = 256
LN_EPS = 1e-5
DEEPNORM_ALPHA = (2.0 * DEPTH) ** 0.25

LANES = 128
BF16_SUBLANES = 16
ROW_SUBLANES = D_MODEL // LANES
TOKEN_SUBLANES = 2 * ROW_SUBLANES
LOG2_E = 1.4426950408889634
ROUTER_GROUP_LANE0 = 0
ROUTER_EXPERT_LANE0 = 8
MOE_ROWS = 512
META_E1, META_E2, META_G1, META_G2 = 0, 1, 2, 3
ROW_TILE = 512
COMBINE_TILE = 256
ISSUE_UNROLL = 8
VMEM_LIMIT = 56 * 1024 * 1024
NEG_INF = float("-inf")


def _params(sem, vmem=VMEM_LIMIT):
    return pltpu.CompilerParams(dimension_semantics=sem, vmem_limit_bytes=vmem)


def _layer_norm(y, g, b):
    mu = jnp.mean(y, axis=-1, keepdims=True)
    yc = y - mu
    var = jnp.mean(yc * yc, axis=-1, keepdims=True)
    return yc * lax.rsqrt(var + LN_EPS) * g + b


def _dot(a, b):
    return jnp.dot(a, b, preferred_element_type=F32)


def _dot_nt(a, b):
    return lax.dot_general(a, b, (((1,), (1,)), ((), ())), preferred_element_type=F32)


def _split_bf16(x):
    hi = x.astype(BF16)
    lo = (x - hi.astype(F32)).astype(BF16)
    return hi, lo


def _qkv_kernel(x_ref, wqk_ref, wvt_ref, qk_ref, vt_ref):
    xb = x_ref[...].astype(BF16)
    for c in range(2):
        acc = _dot(xb, wqk_ref[:, c * D_MODEL:(c + 1) * D_MODEL])
        if c == 0:
            acc = acc * (MOBA_HEAD_DIM ** -0.5 * LOG2_E)
        qk_ref[:, c * D_MODEL:(c + 1) * D_MODEL] = acc.astype(BF16)
    vt_ref[...] = _dot_nt(wvt_ref[...], xb).astype(BF16)


def _qkv_proj(x, wqk_bf16, wvt_bf16):
    T = x.shape[0]
    return pl.pallas_call(
        _qkv_kernel,
        grid=(T // ROW_TILE,),
        in_specs=[pl.BlockSpec((ROW_TILE, D_MODEL), lambda i: (i, 0)),
                  pl.BlockSpec((D_MODEL, 2 * D_MODEL), lambda i: (0, 0)),
                  pl.BlockSpec((D_MODEL, D_MODEL), lambda i: (0, 0))],
        out_specs=[pl.BlockSpec((ROW_TILE, 2 * D_MODEL), lambda i: (i, 0)),
                   pl.BlockSpec((D_MODEL, ROW_TILE), lambda i: (0, i))],
        out_shape=[jax.ShapeDtypeStruct((T, 2 * D_MODEL), BF16),
                   jax.ShapeDtypeStruct((D_MODEL, T), BF16)],
        compiler_params=_params(("parallel",)),
        name="moba_qkv",
    )(x, wqk_bf16, wvt_bf16)


def _moba_kernel(q_ref, k_ref, vt_ref, o_ref, s_scr, p_scr, vt_scr, *, n_blocks):
    KB = MOBA_BLOCK
    S = n_blocks * KB
    vt_scr[:LANES, :] = vt_ref[...]
    vt_scr[LANES:, :] = jnp.ones((vt_scr.shape[0] - LANES, S), BF16)
    q = q_ref[...]
    lane = lax.broadcasted_iota(jnp.int32, (S, LANES), 1)
    blk = lax.broadcasted_iota(jnp.int32, (n_blocks, S), 0)
    cur = jnp.right_shift(lax.broadcasted_iota(jnp.int32, (n_blocks, S), 1),
                          KB.bit_length() - 1)
    key_i = lax.broadcasted_iota(jnp.int32, (KB, KB), 0)
    qry_i = lax.broadcasted_iota(jnp.int32, (KB, KB), 1)
    drow = lax.broadcasted_iota(jnp.int32, (LANES, KB), 0)

    kf = k_ref[...].astype(F32).reshape(n_blocks, KB, LANES)
    kmean = jnp.sum(kf, axis=1) * (1.0 / KB)
    km_hi, km_lo = _split_bf16(kmean)

    qh, bias = [], []
    for h in range(2):
        head_lanes = (lane < MOBA_HEAD_DIM) if h == 0 else (lane >= MOBA_HEAD_DIM)
        qh_h = jnp.where(head_lanes, q, jnp.zeros_like(q))
        bs = _dot_nt(km_hi, qh_h) + _dot_nt(km_lo, qh_h)
        cnt = jnp.zeros((n_blocks, S), jnp.int32)
        for jp in range(n_blocks):
            cj = bs[jp:jp + 1, :]
            beats = (cj > bs) | ((cj == bs) & (jp < blk))
            cnt = cnt + jnp.where(beats & (jp < cur), 1, 0)
        qh.append(qh_h)
        bias.append(jnp.where((blk < cur) & (cnt < MOBA_TOPK), 0.0, NEG_INF))

    steps = [(i, h) for i in range(n_blocks) for h in range(2)]
    n_slots = s_scr.shape[0]
    state = [dict(m=None, acc=None) for _ in steps]

    def block_bias(n, j):
        i, h = steps[n]
        return bias[h][j:j + 1, i * KB:(i + 1) * KB]

    def pass1(n):
        i, h = steps[n]
        qi = qh[h][i * KB:(i + 1) * KB]
        s_all = _dot_nt(k_ref[0:(i + 1) * KB, :], qi)
        m = None
        for j in range(i + 1):
            s = s_all[j * KB:(j + 1) * KB, :]
            if j == i:
                s = jnp.where(key_i <= qry_i, s, NEG_INF)
            s_scr[n % n_slots, j * KB:(j + 1) * KB, :] = s
            cm = jnp.max(s, axis=0, keepdims=True)
            if j < i:
                cm = cm + block_bias(n, j)
            m = cm if m is None else jnp.maximum(m, cm)
        state[n]["m"] = m

    def pass2(n):
        i, _ = steps[n]
        m = state[n]["m"]
        for j in range(i + 1):
            shift = -m if j == i else block_bias(n, j) - m
            p = jnp.exp2(s_scr[n % n_slots, j * KB:(j + 1) * KB, :] + shift)
            p_scr[n % n_slots, j * KB:(j + 1) * KB, :] = p.astype(BF16)
        state[n]["acc"] = _dot(vt_scr[:, 0:(i + 1) * KB],
                               p_scr[n % n_slots, 0:(i + 1) * KB, :])

    def finish(n):
        i, h = steps[n]
        if h == 1:
            a0, a1 = state[n - 1]["acc"], state[n]["acc"]
            o0 = a0[:LANES] / a0[LANES:LANES + 1]
            o1 = a1[:LANES] / a1[LANES:LANES + 1]
            ot = jnp.where(drow < MOBA_HEAD_DIM, o0, o1)
            o_ref[i * KB:(i + 1) * KB, :] = ot.T.astype(BF16)

    pass1(0)
    for n in range(len(steps)):
        if n + 1 < len(steps):
            pass1(n + 1)
        pass2(n)
        finish(n)


def _moba_attention(qk, vt, batch, seq):
    T = batch * seq
    nb = seq // MOBA_BLOCK
    n_pairs = MOBA_HEADS // 2
    return pl.pallas_call(
        functools.partial(_moba_kernel, n_blocks=nb),
        grid=(batch, n_pairs),
        in_specs=[pl.BlockSpec((seq, LANES), lambda b, hp: (b, hp)),
                  pl.BlockSpec((seq, LANES), lambda b, hp: (b, n_pairs + hp)),
                  pl.BlockSpec((LANES, seq), lambda b, hp: (hp, b))],
        out_specs=pl.BlockSpec((seq, LANES), lambda b, hp: (b, hp)),
        out_shape=jax.ShapeDtypeStruct((T, D_MODEL), BF16),
        scratch_shapes=[pltpu.VMEM((4, seq, MOBA_BLOCK), F32),
                        pltpu.VMEM((4, seq, MOBA_BLOCK), BF16),
                        pltpu.VMEM((LANES + BF16_SUBLANES, seq), BF16)],
        compiler_params=_params(("parallel", "parallel")),
        name="moba_attn",
    )(qk, qk, vt)


def _proj_ln_kernel(a_ref, w_ref, x_ref, g_ref, b_ref, o_ref):
    y = DEEPNORM_ALPHA * x_ref[...] + _dot(a_ref[...], w_ref[...])
    o_ref[...] = _layer_norm(y, g_ref[...], b_ref[...])


def _proj_ln(a, w_bf16, x, g, b):
    T = x.shape[0]
    row = lambda i: (i, 0)
    fixed = lambda i: (0, 0)
    return pl.pallas_call(
        _proj_ln_kernel,
        grid=(T // ROW_TILE,),
        in_specs=[pl.BlockSpec((ROW_TILE, D_MODEL), row),
                  pl.BlockSpec((D_MODEL, D_MODEL), fixed),
                  pl.BlockSpec((ROW_TILE, D_MODEL), row),
                  pl.BlockSpec((1, D_MODEL), fixed),
                  pl.BlockSpec((1, D_MODEL), fixed)],
        out_specs=pl.BlockSpec((ROW_TILE, D_MODEL), row),
        out_shape=jax.ShapeDtypeStruct((T, D_MODEL), F32),
        compiler_params=_params(("parallel",)),
        name="moba_out_ln",
    )(a, w_bf16, x, g.reshape(1, -1), b.reshape(1, -1))


def _gelu(z):
    return 0.5 * z * (1.0 + lax.erf(z * (2.0 ** -0.5)))


def _gmlp_kernel(x_ref, win_ref, bin_ref, lg_ref, lb_ref, ws_ref, bs_ref, wout_ref,
                 g1_ref, b1_ref, o_ref, u_scr, v_scr, gate_scr):
    tm = x_ref.shape[0]
    cw = GMLP_DV // 4
    xb = x_ref[...].astype(BF16)
    for c in range(8):
        z = _dot(xb, win_ref[:, c * cw:(c + 1) * cw]) + bin_ref[:, c * cw:(c + 1) * cw]
        z = _gelu(z)
        if c < 4:
            u_scr[:, c * cw:(c + 1) * cw] = z.astype(BF16)
        else:
            v_scr[:, (c - 4) * cw:(c - 3) * cw] = z
    vn = _layer_norm(v_scr[...], lg_ref[...], lb_ref[...]).astype(BF16)

    row = lax.broadcasted_iota(jnp.int32, (GMLP_CHUNK, GMLP_CHUNK), 0)
    col = lax.broadcasted_iota(jnp.int32, (GMLP_CHUNK, GMLP_CHUNK), 1)
    gd = GMLP_GROUP_DIM
    for g in range(GMLP_GROUPS):
        wc = jnp.where(col <= row, ws_ref[g], 0.0).astype(BF16)
        bias = bs_ref[:, g:g + 1]
        for c in range(tm // GMLP_CHUNK):
            r0 = c * GMLP_CHUNK
            mixed = _dot(wc, vn[r0:r0 + GMLP_CHUNK, g * gd:(g + 1) * gd]) + bias
            u = u_scr[r0:r0 + GMLP_CHUNK, g * gd:(g + 1) * gd].astype(F32)
            gate_scr[r0:r0 + GMLP_CHUNK, g * gd:(g + 1) * gd] = (u * mixed).astype(BF16)
    y = DEEPNORM_ALPHA * x_ref[...] + _dot(gate_scr[...], wout_ref[...])
    o_ref[...] = _layer_norm(y, g1_ref[...], b1_ref[...])


def _gmlp_layer(x, w_in, b_in, ln_g, ln_b, w_s, b_s_t, w_out, g1, b1):
    T = x.shape[0]
    row = lambda i: (i, 0)
    fixed = lambda i: (0, 0)
    once = pl.Buffered(1)
    return pl.pallas_call(
        _gmlp_kernel,
        grid=(T // ROW_TILE,),
        in_specs=[pl.BlockSpec((ROW_TILE, D_MODEL), row),
                  pl.BlockSpec((D_MODEL, 2 * GMLP_DV), fixed, pipeline_mode=once),
                  pl.BlockSpec((1, 2 * GMLP_DV), fixed),
                  pl.BlockSpec((1, GMLP_DV), fixed),
                  pl.BlockSpec((1, GMLP_DV), fixed),
                  pl.BlockSpec((GMLP_GROUPS, GMLP_CHUNK, GMLP_CHUNK), lambda i: (0, 0, 0)),
                  pl.BlockSpec((GMLP_CHUNK, GMLP_GROUPS), fixed),
                  pl.BlockSpec((GMLP_DV, D_MODEL), fixed, pipeline_mode=once),
                  pl.BlockSpec((1, D_MODEL), fixed),
                  pl.BlockSpec((1, D_MODEL), fixed)],
        out_specs=pl.BlockSpec((ROW_TILE, D_MODEL), row),
        out_shape=jax.ShapeDtypeStruct((T, D_MODEL), F32),
        scratch_shapes=[pltpu.VMEM((ROW_TILE, GMLP_DV), BF16),
                        pltpu.VMEM((ROW_TILE, GMLP_DV), F32),
                        pltpu.VMEM((ROW_TILE, GMLP_DV), BF16)],
        compiler_params=_params(("parallel",)),
        name="gmlp",
    )(x, w_in, b_in.reshape(1, -1), ln_g.reshape(1, -1), ln_b.reshape(1, -1), w_s, b_s_t,
      w_out, g1.reshape(1, -1), b1.reshape(1, -1))


def _router_kernel(x_ref, w_ref, b_ref, ids_ref, meta_ref, cnt_ref, carry):
    tm = x_ref.shape[0]

    @pl.when(pl.program_id(0) == 0)
    def _():
        carry[...] = jnp.zeros_like(carry)

    xh, xl = _split_bf16(x_ref[...])
    wh, wl = _split_bf16(w_ref[...])
    logits = _dot(xh, wh) + (_dot(xl, wh) + _dot(xh, wl)) + b_ref[...]
    lane = lax.broadcasted_iota(jnp.int32, (tm, LANES), 1)

    def first_max(vals):
        top = jnp.max(vals, axis=-1, keepdims=True)
        idx = jnp.min(jnp.where(vals == top, lane, LANES), axis=-1, keepdims=True)
        return top, idx

    gl = jnp.where(lane < ROUTER_EXPERT_LANE0, logits, NEG_INF)
    g_top, g_idx = first_max(gl)
    g_p = 1.0 / jnp.sum(jnp.exp(gl - g_top), axis=-1, keepdims=True)
    lo = ROUTER_EXPERT_LANE0 + g_idx * MOE_EXPERTS_PER_GROUP
    el = jnp.where((lane >= lo) & (lane < lo + MOE_EXPERTS_PER_GROUP), logits, NEG_INF)
    e1, i1 = first_max(el)
    e2, i2 = first_max(jnp.where(lane == i1, NEG_INF, el))
    d = jnp.exp(e2 - e1)
    gate1 = g_p / (1.0 + d)
    gate2 = g_p * d / (1.0 + d)

    r = lax.broadcasted_iota(jnp.int32, (tm, tm), 0)
    c = lax.broadcasted_iota(jnp.int32, (tm, tm), 1)
    before = jnp.where(c < r, 1.0, 0.0).astype(BF16)
    oh = lane == g_idx
    ohf = jnp.where(oh, 1.0, 0.0)
    ahead = _dot(before, ohf.astype(BF16))
    base = carry[...]
    rank = jnp.sum(jnp.where(oh, base + ahead, 0.0), axis=-1, keepdims=True)
    total = base + jnp.sum(ohf, axis=0, keepdims=True)
    carry[...] = total
    cnt_ref[...] = jnp.broadcast_to(total, cnt_ref.shape)

    ids_ref[...] = jnp.where(lane == 0, g_idx, jnp.where(lane == 1, rank.astype(jnp.int32), 0))
    meta_ref[...] = jnp.where(lane == META_E1, (i1 - lo).astype(F32),
                    jnp.where(lane == META_E2, (i2 - lo).astype(F32),
                    jnp.where(lane == META_G1, gate1,
                    jnp.where(lane == META_G2, gate2, 0.0))))


def _router(x, w_pad, b_pad):
    T = x.shape[0]
    row = lambda i: (i, 0)
    fixed = lambda i: (0, 0)
    return pl.pallas_call(
        _router_kernel,
        grid=(T // ROW_TILE,),
        in_specs=[pl.BlockSpec((ROW_TILE, D_MODEL), row),
                  pl.BlockSpec((D_MODEL, LANES), fixed),
                  pl.BlockSpec((1, LANES), fixed)],
        out_specs=[pl.BlockSpec((ROW_TILE, LANES), row),
                   pl.BlockSpec((ROW_TILE, LANES), row),
                   pl.BlockSpec((8, LANES), fixed)],
        out_shape=[jax.ShapeDtypeStruct((T, LANES), jnp.int32),
                   jax.ShapeDtypeStruct((T, LANES), F32),
                   jax.ShapeDtypeStruct((8, LANES), F32)],
        scratch_shapes=[pltpu.VMEM((1, LANES), F32)],
        compiler_params=_params(("arbitrary",)),
        name="moe_router",
    )(x, w_pad, b_pad)


def _tile_row(ref, row, sublanes):
    return ref.at[pl.ds(pl.multiple_of(row * sublanes, sublanes), sublanes)]


def _tile_row_copy(src_ref, src_row, dst_ref, dst_row, sem, sublanes):
    return pltpu.make_async_copy(_tile_row(src_ref, src_row, sublanes),
                                 _tile_row(dst_ref, dst_row, sublanes), sem)


def _tile_rows_wait(src_ref, dst_ref, n_rows, sem, sublanes):
    n = n_rows * sublanes
    pltpu.make_async_copy(src_ref.at[pl.ds(0, n)], dst_ref.at[pl.ds(0, n)], sem).wait()


def _load_tile_rows(ref, n_rows, sublanes):
    return jnp.concatenate([ref[pl.ds(c, n_rows, stride=sublanes), :]
                            for c in range(ROW_SUBLANES)], axis=1)


def _store_tile_rows(ref, val, sublanes):
    for c in range(ROW_SUBLANES):
        ref[pl.ds(c, val.shape[0], stride=sublanes), :] = val[:, c * LANES:(c + 1) * LANES]


def _dispatch_kernel(dest_ref, zero_ref, x_ref, meta_ref, buf_hbm, row_scr, sem):
    tm = x_ref.shape[0]
    ts = TOKEN_SUBLANES
    i = pl.program_id(0)
    n = pl.num_programs(0)
    slot = i % 2

    @pl.when(i == 0)
    def _():
        row_scr[...] = jnp.zeros_like(row_scr)

        def zero_block(b, start):
            @pl.when(zero_ref[b] != 0)
            def _():
                for k in range(MOE_ROWS // tm):
                    rows = pl.ds((b * MOE_ROWS + k * tm) * ts, tm * ts)
                    cp = pltpu.make_async_copy(row_scr.at[1], buf_hbm.at[rows], sem.at[1])
                    if start:
                        cp.start()
                    else:
                        cp.wait()
        for b in range(zero_ref.shape[0]):
            zero_block(b, True)
        for b in range(zero_ref.shape[0]):
            zero_block(b, False)

    stage = row_scr.at[slot]

    @pl.when(i >= 2)
    def _():
        _tile_rows_wait(stage, buf_hbm, tm, sem.at[slot], ts)

    _store_tile_rows(stage, x_ref[...], ts)
    stage[pl.ds(ROW_SUBLANES, tm, stride=ts), :] = meta_ref[...]

    def issue(r, _):
        _tile_row_copy(stage, r, buf_hbm, dest_ref[i * tm + r], sem.at[slot], ts).start()
        return 0
    lax.fori_loop(0, tm, issue, 0, unroll=ISSUE_UNROLL)

    @pl.when(i == n - 1)
    def _():
        _tile_rows_wait(stage, buf_hbm, tm, sem.at[slot], ts)

        @pl.when(n >= 2)
        def _():
            _tile_rows_wait(row_scr.at[1 - slot], buf_hbm, tm, sem.at[1 - slot], ts)


def _dispatch(dest, zero_flag, x, meta, n_rows):
    T = x.shape[0]
    return pl.pallas_call(
        _dispatch_kernel,
        grid_spec=pltpu.PrefetchScalarGridSpec(
            num_scalar_prefetch=2,
            grid=(T // COMBINE_TILE,),
            in_specs=[pl.BlockSpec((COMBINE_TILE, D_MODEL), lambda i, d, z: (i, 0)),
                      pl.BlockSpec((COMBINE_TILE, LANES), lambda i, d, z: (i, 0))],
            out_specs=pl.BlockSpec(memory_space=pl.ANY),
            scratch_shapes=[pltpu.VMEM((2, COMBINE_TILE * TOKEN_SUBLANES, LANES), F32),
                            pltpu.SemaphoreType.DMA((2,))]),
        out_shape=jax.ShapeDtypeStruct((n_rows * TOKEN_SUBLANES, LANES), F32),
        compiler_params=pltpu.CompilerParams(dimension_semantics=("arbitrary",),
                                             has_side_effects=True),
        name="moe_dispatch",
    )(dest, zero_flag, x, meta)

def _ffn_kernel(bg_ref, nused_ref, rows_ref, w1_ref, w3_ref, w2_ref, o_ref):
    del bg_ref
    i = pl.program_id(0)

    @pl.when(i < nused_ref[0])
    def _():
        x = _load_tile_rows(rows_ref, MOE_ROWS, TOKEN_SUBLANES).astype(BF16)
        meta = rows_ref[pl.ds(ROW_SUBLANES, MOE_ROWS, stride=TOKEN_SUBLANES), :]
        e1 = meta[:, META_E1:META_E1 + 1]
        e2 = meta[:, META_E2:META_E2 + 1]
        g1 = meta[:, META_G1:META_G1 + 1]
        g2 = meta[:, META_G2:META_G2 + 1]
        acc = None
        for e in range(MOE_EXPERTS_PER_GROUP):
            h1 = _dot(x, w1_ref[0, e].astype(BF16))
            h3 = _dot(x, w3_ref[0, e].astype(BF16))
            gate = jnp.where(e1 == e, g1, g2)
            h = h1 * (1.0 / (1.0 + jnp.exp(-h1))) * h3 * gate
            h = jnp.where((e1 == e) | (e2 == e), h, 0.0).astype(BF16)
            y = _dot(h, w2_ref[0, e].astype(BF16))
            acc = y if acc is None else acc + y
        _store_tile_rows(o_ref, acc, ROW_SUBLANES)

    @pl.when(i >= nused_ref[0])
    def _():
        o_ref[...] = jnp.zeros_like(o_ref)


def _expert_ffn(blk_group, n_used, buf, w1, w3, w2, layer):
    n_rows = buf.shape[0] // TOKEN_SUBLANES
    epg = MOE_EXPERTS_PER_GROUP
    wsel = lambda i, bg, nu: (layer, bg[i], 0, 0)
    once = pl.Buffered(1)
    return pl.pallas_call(
        _ffn_kernel,
        grid_spec=pltpu.PrefetchScalarGridSpec(
            num_scalar_prefetch=2,
            grid=(n_rows // MOE_ROWS,),
            in_specs=[pl.BlockSpec((MOE_ROWS * TOKEN_SUBLANES, LANES),
                                   lambda i, bg, nu: (jnp.minimum(i, nu[0] - 1), 0)),
                      pl.BlockSpec((1, epg, D_MODEL, MOE_D_EXPERT), wsel, pipeline_mode=once),
                      pl.BlockSpec((1, epg, D_MODEL, MOE_D_EXPERT), wsel, pipeline_mode=once),
                      pl.BlockSpec((1, epg, MOE_D_EXPERT, D_MODEL), wsel, pipeline_mode=once)],
            out_specs=pl.BlockSpec((MOE_ROWS * ROW_SUBLANES, LANES), lambda i, bg, nu: (i, 0))),
        out_shape=jax.ShapeDtypeStruct((n_rows * ROW_SUBLANES, LANES), F32),
        compiler_params=_params(("arbitrary",)),
        name="moe_ffn",
    )(blk_group, n_used, buf, w1, w3, w2)


def _combine_kernel(dest_ref, y_hbm, x_ref, g_ref, b_ref, o_ref, y_scr, sem):
    tm = x_ref.shape[0]
    i = pl.program_id(0)

    def gather(tile, slot):
        def issue(r, _):
            _tile_row_copy(y_hbm, dest_ref[tile * tm + r], y_scr.at[slot], r, sem.at[slot],
                           ROW_SUBLANES).start()
            return 0
        lax.fori_loop(0, tm, issue, 0, unroll=ISSUE_UNROLL)

    @pl.when(i == 0)
    def _():
        gather(0, 0)

    @pl.when(i + 1 < pl.num_programs(0))
    def _():
        gather(i + 1, (i + 1) % 2)

    slot = i % 2
    _tile_rows_wait(y_hbm, y_scr.at[slot], tm, sem.at[slot], ROW_SUBLANES)
    y = DEEPNORM_ALPHA * x_ref[...] + _load_tile_rows(y_scr.at[slot], tm, ROW_SUBLANES)
    o_ref[...] = _layer_norm(y, g_ref[...], b_ref[...])


def _combine_ln(dest, y_buf, x, g, b):
    T = x.shape[0]
    row = lambda i, d: (i, 0)
    fixed = lambda i, d: (0, 0)
    return pl.pallas_call(
        _combine_kernel,
        grid_spec=pltpu.PrefetchScalarGridSpec(
            num_scalar_prefetch=1,
            grid=(T // COMBINE_TILE,),
            in_specs=[pl.BlockSpec(memory_space=pl.ANY),
                      pl.BlockSpec((COMBINE_TILE, D_MODEL), row),
                      pl.BlockSpec((1, D_MODEL), fixed),
                      pl.BlockSpec((1, D_MODEL), fixed)],
            out_specs=pl.BlockSpec((COMBINE_TILE, D_MODEL), row),
            scratch_shapes=[pltpu.VMEM((2, COMBINE_TILE * ROW_SUBLANES, LANES), F32),
                            pltpu.SemaphoreType.DMA((2,))]),
        out_shape=jax.ShapeDtypeStruct((T, D_MODEL), F32),
        compiler_params=_params(("arbitrary",)),
        name="moe_combine_ln",
    )(dest, y_buf, x, g.reshape(1, -1), b.reshape(1, -1))


def _hier_moe_ln(x, w_grp, b_grp, w_rt, b_rt, w1, w3, w2, layer, g2, b2):
    T = x.shape[0]
    pad_w = LANES - MOE_GROUPS - MOE_EXPERTS
    w_pad = jnp.concatenate([w_grp, w_rt, jnp.zeros((D_MODEL, pad_w), F32)], axis=1)
    b_pad = jnp.concatenate([b_grp, b_rt, jnp.zeros((pad_w,), F32)]).reshape(1, LANES)
    ids, meta, cnt = _router(x, w_pad, b_pad)

    counts = cnt[0, ROUTER_GROUP_LANE0:ROUTER_GROUP_LANE0 + MOE_GROUPS].astype(jnp.int32)
    padded = ((counts + MOE_ROWS - 1) // MOE_ROWS) * MOE_ROWS
    pad_end = jnp.cumsum(padded)
    pad_start = pad_end - padded
    dest = pad_start[ids[:, 0]] + ids[:, 1]
    n_blk = T // MOE_ROWS + MOE_GROUPS
    n_used = (pad_end[-1:] // MOE_ROWS).astype(jnp.int32)
    blk_row0 = jnp.minimum(jnp.arange(n_blk), n_used - 1) * MOE_ROWS
    blk_group = jnp.sum(pad_end[None, :] <= blk_row0[:, None], axis=1).astype(jnp.int32)
    blk_end = (jnp.arange(n_blk) + 1) * MOE_ROWS
    group_last = jnp.any((blk_end[:, None] == pad_end[None, :]) & (padded[None, :] > 0), axis=1)
    zero_flag = (group_last | (jnp.arange(n_blk) >= n_used)).astype(jnp.int32)

    buf = _dispatch(dest, zero_flag, x, meta, n_blk * MOE_ROWS)
    y_buf = _expert_ffn(blk_group, n_used, buf, w1, w3, w2, layer)
    return _combine_ln(dest, y_buf, x, g2, b2)


def kernel(x, moba_w_qkv, moba_w_o, gmlp_w_in, gmlp_b_in, gmlp_ln_g, gmlp_ln_b, gmlp_w_s,
           gmlp_b_s, gmlp_w_out, ln1_g, ln1_b, ln2_g, ln2_b, moe_w_grp, moe_b_grp, moe_w_rt,
           moe_b_rt, moe_w1, moe_w3, moe_w2):
    B, S, D = x.shape
    assert D == D_MODEL and S % MOBA_BLOCK == 0 and (B * S) % ROW_TILE == 0
    xf = x.reshape(B * S, D)
    for i in range(DEPTH):
        j = i // 2
        if i % 2 == 0:
            wqk = moba_w_qkv[j, :, :2 * D].astype(BF16)
            wvt = moba_w_qkv[j, :, 2 * D:].T.astype(BF16)
            qk, vt = _qkv_proj(xf, wqk, wvt)
            att = _moba_attention(qk, vt, B, S)
            xf = _proj_ln(att, moba_w_o[j].astype(BF16), xf, ln1_g[i], ln1_b[i])
        else:
            xf = _gmlp_layer(xf, gmlp_w_in[j].astype(BF16), gmlp_b_in[j], gmlp_ln_g[j],
                             gmlp_ln_b[j], gmlp_w_s[j], gmlp_b_s[j].T,
                             gmlp_w_out[j].astype(BF16), ln1_g[i], ln1_b[i])
        xf = _hier_moe_ln(xf, moe_w_grp[i], moe_b_grp[i], moe_w_rt[i], moe_b_rt[i],
                          moe_w1, moe_w3, moe_w2, i, ln2_g[i], ln2_b[i])
    return xf.reshape(B, S, D)
```

```python
import functools

import jax
import jax.numpy as jnp
from jax import lax
from jax.experimental import pallas as pl
from jax.experimental.pallas import tpu as pltpu

F32 = jnp.float32
BF16 = jnp.bfloat16

D_MODEL = 1024
DEPTH = 4
MOBA_HEADS = 16
MOBA_HEAD_DIM = 64
MOBA_BLOCK = 256
MOBA_TOPK = 3
GMLP_DV = 3072
GMLP_GROUPS = 8
GMLP_GROUP_DIM = GMLP_DV // GMLP_GROUPS
GMLP_CHUNK = 128
MOE_GROUPS = 8
MOE_EXPERTS_PER_GROUP = 8
MOE_EXPERTS = 64
MOE_D_EXPERT = 256
LN_EPS = 1e-5
DEEPNORM_ALPHA = (2.0 * DEPTH) ** 0.25

LANES = 128
BF16_SUBLANES = 16
ROW_SUBLANES = D_MODEL // LANES
TOKEN_SUBLANES = 2 * ROW_SUBLANES
LOG2_E = 1.4426950408889634
ROUTER_GROUP_LANE0 = 0
ROUTER_EXPERT_LANE0 = 8
MOE_ROWS = 512
META_E1, META_E2, META_G1, META_G2 = 0, 1, 2, 3
ROW_TILE = 512
COMBINE_TILE = 256
ISSUE_UNROLL = 8
VMEM_LIMIT = 56 * 1024 * 1024
FFN_VMEM_LIMIT = 62 * 1024 * 1024
NEG_INF = float("-inf")


def _params(sem, vmem=VMEM_LIMIT):
    return pltpu.CompilerParams(dimension_semantics=sem, vmem_limit_bytes=vmem)


def _layer_norm(y, g, b):
    mu = jnp.mean(y, axis=-1, keepdims=True)
    yc = y - mu
    var = jnp.mean(yc * yc, axis=-1, keepdims=True)
    return yc * lax.rsqrt(var + LN_EPS) * g + b


def _dot(a, b):
    return jnp.dot(a, b, preferred_element_type=F32)


def _dot_nt(a, b):
    return lax.dot_general(a, b, (((1,), (1,)), ((), ())), preferred_element_type=F32)


def _split_bf16(x):
    hi = x.astype(BF16)
    lo = (x - hi.astype(F32)).astype(BF16)
    return hi, lo


def _qkv_kernel(x_ref, wqk_ref, wvt_ref, qk_ref, vt_ref):
    xb = x_ref[...].astype(BF16)
    for c in range(2):
        acc = _dot(xb, wqk_ref[:, c * D_MODEL:(c + 1) * D_MODEL])
        if c == 0:
            acc = acc * (MOBA_HEAD_DIM ** -0.5 * LOG2_E)
        qk_ref[:, c * D_MODEL:(c + 1) * D_MODEL] = acc.astype(BF16)
    vt_ref[...] = _dot_nt(wvt_ref[...], xb).astype(BF16)


def _qkv_proj(x, wqk_bf16, wvt_bf16):
    T = x.shape[0]
    return pl.pallas_call(
        _qkv_kernel,
        grid=(T // ROW_TILE,),
        in_specs=[pl.BlockSpec((ROW_TILE, D_MODEL), lambda i: (i, 0)),
                  pl.BlockSpec((D_MODEL, 2 * D_MODEL), lambda i: (0, 0)),
                  pl.BlockSpec((D_MODEL, D_MODEL), lambda i: (0, 0))],
        out_specs=[pl.BlockSpec((ROW_TILE, 2 * D_MODEL), lambda i: (i, 0)),
                   pl.BlockSpec((D_MODEL, ROW_TILE), lambda i: (0, i))],
        out_shape=[jax.ShapeDtypeStruct((T, 2 * D_MODEL), BF16),
                   jax.ShapeDtypeStruct((D_MODEL, T), BF16)],
        compiler_params=_params(("parallel",)),
        name="moba_qkv",
    )(x, wqk_bf16, wvt_bf16)


def _moba_kernel(q_ref, k_ref, vt_ref, o_ref, s_scr, p_scr, vt_scr, *, n_blocks):
    KB = MOBA_BLOCK
    S = n_blocks * KB
    hd = MOBA_HEAD_DIM
    for h in range(2):
        vt_scr[h, :hd, :] = vt_ref[h * hd:(h + 1) * hd, :]
        vt_scr[h, hd:, :] = jnp.ones((vt_scr.shape[1] - hd, S), BF16)
    q = q_ref[...]
    lane = lax.broadcasted_iota(jnp.int32, (S, LANES), 1)
    blk = lax.broadcasted_iota(jnp.int32, (n_blocks, S), 0)
    cur = jnp.right_shift(lax.broadcasted_iota(jnp.int32, (n_blocks, S), 1),
                          KB.bit_length() - 1)
    key_i = lax.broadcasted_iota(jnp.int32, (KB, KB), 0)
    qry_i = lax.broadcasted_iota(jnp.int32, (KB, KB), 1)

    kf = k_ref[...].astype(F32).reshape(n_blocks, KB, LANES)
    kmean = jnp.sum(kf, axis=1) * (1.0 / KB)
    km_hi, km_lo = _split_bf16(kmean)

    qh, bias = [], []
    for h in range(2):
        head_lanes = (lane < MOBA_HEAD_DIM) if h == 0 else (lane >= MOBA_HEAD_DIM)
        qh_h = jnp.where(head_lanes, q, jnp.zeros_like(q))
        bs = _dot_nt(km_hi, qh_h) + _dot_nt(km_lo, qh_h)
        cnt = jnp.zeros((n_blocks, S), jnp.int32)
        for jp in range(n_blocks):
            cj = bs[jp:jp + 1, :]
            beats = (cj > bs) | ((cj == bs) & (jp < blk))
            cnt = cnt + jnp.where(beats & (jp < cur), 1, 0)
        qh.append(qh_h)
        bias.append(jnp.where((blk < cur) & (cnt < MOBA_TOPK), 0.0, NEG_INF))

    steps = [(i, h) for i in range(n_blocks) for h in range(2)]
    n_slots = s_scr.shape[0]
    state = [dict(m=None, acc=None) for _ in steps]

    def block_bias(n, j):
        i, h = steps[n]
        return bias[h][j:j + 1, i * KB:(i + 1) * KB]

    def scores(n):
        i, h = steps[n]
        qi = qh[h][i * KB:(i + 1) * KB]
        state[n]["s"] = _dot_nt(k_ref[0:(i + 1) * KB, :], qi)

    def pass1_block(n, j):
        i, _ = steps[n]
        st = state[n]
        s = st["s"][j * KB:(j + 1) * KB, :]
        if j == i:
            s = jnp.where(key_i <= qry_i, s, NEG_INF)
        s_scr[n % n_slots, j * KB:(j + 1) * KB, :] = s
        cm = jnp.max(s, axis=0, keepdims=True)
        if j < i:
            cm = cm + block_bias(n, j)
        st["m"] = cm if st["m"] is None else jnp.maximum(st["m"], cm)

    def pass2_block(n, j):
        i, _ = steps[n]
        m = state[n]["m"]
        shift = -m if j == i else block_bias(n, j) - m
        p = jnp.exp2(s_scr[n % n_slots, j * KB:(j + 1) * KB, :] + shift)
        p_scr[n % n_slots, j * KB:(j + 1) * KB, :] = p.astype(BF16)

    def finish(n):
        i, h = steps[n]
        state[n]["acc"] = _dot(vt_scr[h, :, 0:(i + 1) * KB],
                               p_scr[n % n_slots, 0:(i + 1) * KB, :])
        state[n]["s"] = None
        if h == 1:
            a0, a1 = state[n - 1]["acc"], state[n]["acc"]
            ot = jnp.concatenate([a0[:hd] / a0[hd:hd + 1], a1[:hd] / a1[hd:hd + 1]], axis=0)
            o_ref[i * KB:(i + 1) * KB, :] = ot.T.astype(BF16)

    scores(0)
    for j in range(steps[0][0] + 1):
        pass1_block(0, j)
    for n in range(len(steps)):
        cur_blocks = steps[n][0] + 1
        nxt_blocks = steps[n + 1][0] + 1 if n + 1 < len(steps) else 0
        if nxt_blocks:
            scores(n + 1)
        for j in range(max(cur_blocks, nxt_blocks)):
            if j < nxt_blocks:
                pass1_block(n + 1, j)
            if j < cur_blocks:
                pass2_block(n, j)
        finish(n)


def _moba_attention(qk, vt, batch, seq):
    T = batch * seq
    nb = seq // MOBA_BLOCK
    n_pairs = MOBA_HEADS // 2
    return pl.pallas_call(
        functools.partial(_moba_kernel, n_blocks=nb),
        grid=(batch, n_pairs),
        in_specs=[pl.BlockSpec((seq, LANES), lambda b, hp: (b, hp)),
                  pl.BlockSpec((seq, LANES), lambda b, hp: (b, n_pairs + hp)),
                  pl.BlockSpec((LANES, seq), lambda b, hp: (hp, b))],
        out_specs=pl.BlockSpec((seq, LANES), lambda b, hp: (b, hp)),
        out_shape=jax.ShapeDtypeStruct((T, D_MODEL), BF16),
        scratch_shapes=[pltpu.VMEM((4, seq, MOBA_BLOCK), F32),
                        pltpu.VMEM((4, seq, MOBA_BLOCK), BF16),
                        pltpu.VMEM((2, MOBA_HEAD_DIM + BF16_SUBLANES, seq), BF16)],
        compiler_params=_params(("parallel", "parallel")),
        name="moba_attn",
    )(qk, qk, vt)


def _proj_ln_kernel(a_ref, w_ref, x_ref, g_ref, b_ref, o_ref):
    y = DEEPNORM_ALPHA * x_ref[...] + _dot(a_ref[...], w_ref[...])
    o_ref[...] = _layer_norm(y, g_ref[...], b_ref[...])


def _proj_ln(a, w_bf16, x, g, b):
    T = x.shape[0]
    row = lambda i: (i, 0)
    fixed = lambda i: (0, 0)
    return pl.pallas_call(
        _proj_ln_kernel,
        grid=(T // ROW_TILE,),
        in_specs=[pl.BlockSpec((ROW_TILE, D_MODEL), row),
                  pl.BlockSpec((D_MODEL, D_MODEL), fixed),
                  pl.BlockSpec((ROW_TILE, D_MODEL), row),
                  pl.BlockSpec((1, D_MODEL), fixed),
                  pl.BlockSpec((1, D_MODEL), fixed)],
        out_specs=pl.BlockSpec((ROW_TILE, D_MODEL), row),
        out_shape=jax.ShapeDtypeStruct((T, D_MODEL), F32),
        compiler_params=_params(("parallel",)),
        name="moba_out_ln",
    )(a, w_bf16, x, g.reshape(1, -1), b.reshape(1, -1))


def _gelu(z):
    return 0.5 * z * (1.0 + lax.erf(z * (2.0 ** -0.5)))


def _gmlp_kernel(x_ref, win_ref, bin_ref, lg_ref, lb_ref, ws_ref, bs_ref, wout_ref,
                 g1_ref, b1_ref, o_ref, u_scr, v_scr, gate_scr):
    tm = x_ref.shape[0]
    cw = GMLP_DV // 4
    xb = x_ref[...].astype(BF16)
    for c in range(8):
        z = _dot(xb, win_ref[:, c * cw:(c + 1) * cw]) + bin_ref[:, c * cw:(c + 1) * cw]
        z = _gelu(z)
        if c < 4:
            u_scr[:, c * cw:(c + 1) * cw] = z.astype(BF16)
        else:
            v_scr[:, (c - 4) * cw:(c - 3) * cw] = z
    vn = _layer_norm(v_scr[...], lg_ref[...], lb_ref[...]).astype(BF16)

    row = lax.broadcasted_iota(jnp.int32, (GMLP_CHUNK, GMLP_CHUNK), 0)
    col = lax.broadcasted_iota(jnp.int32, (GMLP_CHUNK, GMLP_CHUNK), 1)
    gd = GMLP_GROUP_DIM
    for g in range(GMLP_GROUPS):
        wc = jnp.where(col <= row, ws_ref[g], 0.0).astype(BF16)
        bias = bs_ref[:, g:g + 1]
        for c in range(tm // GMLP_CHUNK):
            r0 = c * GMLP_CHUNK
            mixed = _dot(wc, vn[r0:r0 + GMLP_CHUNK, g * gd:(g + 1) * gd]) + bias
            u = u_scr[r0:r0 + GMLP_CHUNK, g * gd:(g + 1) * gd].astype(F32)
            gate_scr[r0:r0 + GMLP_CHUNK, g * gd:(g + 1) * gd] = (u * mixed).astype(BF16)
    y = DEEPNORM_ALPHA * x_ref[...] + _dot(gate_scr[...], wout_ref[...])
    o_ref[...] = _layer_norm(y, g1_ref[...], b1_ref[...])


def _gmlp_layer(x, w_in, b_in, ln_g, ln_b, w_s, b_s_t, w_out, g1, b1):
    T = x.shape[0]
    row = lambda i: (i, 0)
    fixed = lambda i: (0, 0)
    once = pl.Buffered(1)
    return pl.pallas_call(
        _gmlp_kernel,
        grid=(T // ROW_TILE,),
        in_specs=[pl.BlockSpec((ROW_TILE, D_MODEL), row),
                  pl.BlockSpec((D_MODEL, 2 * GMLP_DV), fixed, pipeline_mode=once),
                  pl.BlockSpec((1, 2 * GMLP_DV), fixed),
                  pl.BlockSpec((1, GMLP_DV), fixed),
                  pl.BlockSpec((1, GMLP_DV), fixed),
                  pl.BlockSpec((GMLP_GROUPS, GMLP_CHUNK, GMLP_CHUNK), lambda i: (0, 0, 0)),
                  pl.BlockSpec((GMLP_CHUNK, GMLP_GROUPS), fixed),
                  pl.BlockSpec((GMLP_DV, D_MODEL), fixed, pipeline_mode=once),
                  pl.BlockSpec((1, D_MODEL), fixed),
                  pl.BlockSpec((1, D_MODEL), fixed)],
        out_specs=pl.BlockSpec((ROW_TILE, D_MODEL), row),
        out_shape=jax.ShapeDtypeStruct((T, D_MODEL), F32),
        scratch_shapes=[pltpu.VMEM((ROW_TILE, GMLP_DV), BF16),
                        pltpu.VMEM((ROW_TILE, GMLP_DV), F32),
                        pltpu.VMEM((ROW_TILE, GMLP_DV), BF16)],
        compiler_params=_params(("parallel",)),
        name="gmlp",
    )(x, w_in, b_in.reshape(1, -1), ln_g.reshape(1, -1), ln_b.reshape(1, -1), w_s, b_s_t,
      w_out, g1.reshape(1, -1), b1.reshape(1, -1))


def _router_kernel(x_ref, w_ref, b_ref, ids_ref, meta_ref, cnt_ref, carry):
    tm = x_ref.shape[0]

    @pl.when(pl.program_id(0) == 0)
    def _():
        carry[...] = jnp.zeros_like(carry)

    xh, xl = _split_bf16(x_ref[...])
    wh, wl = _split_bf16(w_ref[...])
    logits = _dot(xh, wh) + (_dot(xl, wh) + _dot(xh, wl)) + b_ref[...]
    lane = lax.broadcasted_iota(jnp.int32, (tm, LANES), 1)

    def first_max(vals):
        top = jnp.max(vals, axis=-1, keepdims=True)
        idx = jnp.min(jnp.where(vals == top, lane, LANES), axis=-1, keepdims=True)
        return top, idx

    gl = jnp.where(lane < ROUTER_EXPERT_LANE0, logits, NEG_INF)
    g_top, g_idx = first_max(gl)
    g_p = 1.0 / jnp.sum(jnp.exp(gl - g_top), axis=-1, keepdims=True)
    lo = ROUTER_EXPERT_LANE0 + g_idx * MOE_EXPERTS_PER_GROUP
    el = jnp.where((lane >= lo) & (lane < lo + MOE_EXPERTS_PER_GROUP), logits, NEG_INF)
    e1, i1 = first_max(el)
    e2, i2 = first_max(jnp.where(lane == i1, NEG_INF, el))
    d = jnp.exp(e2 - e1)
    gate1 = g_p / (1.0 + d)
    gate2 = g_p * d / (1.0 + d)

    r = lax.broadcasted_iota(jnp.int32, (tm, tm), 0)
    c = lax.broadcasted_iota(jnp.int32, (tm, tm), 1)
    before = jnp.where(c < r, 1.0, 0.0).astype(BF16)
    oh = lane == g_idx
    ohf = jnp.where(oh, 1.0, 0.0)
    ahead = _dot(before, ohf.astype(BF16))
    base = carry[...]
    rank = jnp.sum(jnp.where(oh, base + ahead, 0.0), axis=-1, keepdims=True)
    total = base + jnp.sum(ohf, axis=0, keepdims=True)
    carry[...] = total
    cnt_ref[...] = jnp.broadcast_to(total, cnt_ref.shape)

    ids_ref[...] = jnp.where(lane == 0, g_idx, jnp.where(lane == 1, rank.astype(jnp.int32), 0))
    meta_ref[...] = jnp.where(lane == META_E1, (i1 - lo).astype(F32),
                    jnp.where(lane == META_E2, (i2 - lo).astype(F32),
                    jnp.where(lane == META_G1, gate1,
                    jnp.where(lane == META_G2, gate2, 0.0))))


def _router(x, w_pad, b_pad):
    T = x.shape[0]
    row = lambda i: (i, 0)
    fixed = lambda i: (0, 0)
    return pl.pallas_call(
        _router_kernel,
        grid=(T // ROW_TILE,),
        in_specs=[pl.BlockSpec((ROW_TILE, D_MODEL), row),
                  pl.BlockSpec((D_MODEL, LANES), fixed),
                  pl.BlockSpec((1, LANES), fixed)],
        out_specs=[pl.BlockSpec((ROW_TILE, LANES), row),
                   pl.BlockSpec((ROW_TILE, LANES), row),
                   pl.BlockSpec((8, LANES), fixed)],
        out_shape=[jax.ShapeDtypeStruct((T, LANES), jnp.int32),
                   jax.ShapeDtypeStruct((T, LANES), F32),
                   jax.ShapeDtypeStruct((8, LANES), F32)],
        scratch_shapes=[pltpu.VMEM((1, LANES), F32)],
        compiler_params=_params(("arbitrary",)),
        name="moe_router",
    )(x, w_pad, b_pad)


def _tile_row(ref, row, sublanes):
    return ref.at[pl.ds(pl.multiple_of(row * sublanes, sublanes), sublanes)]


def _tile_row_copy(src_ref, src_row, dst_ref, dst_row, sem, sublanes):
    return pltpu.make_async_copy(_tile_row(src_ref, src_row, sublanes),
                                 _tile_row(dst_ref, dst_row, sublanes), sem)


def _tile_rows_wait(src_ref, dst_ref, n_rows, sem, sublanes):
    n = n_rows * sublanes
    pltpu.make_async_copy(src_ref.at[pl.ds(0, n)], dst_ref.at[pl.ds(0, n)], sem).wait()


def _load_tile_rows(ref, n_rows, sublanes):
    return jnp.concatenate([ref[pl.ds(c, n_rows, stride=sublanes), :]
                            for c in range(ROW_SUBLANES)], axis=1)


def _store_tile_rows(ref, val, sublanes):
    for c in range(ROW_SUBLANES):
        ref[pl.ds(c, val.shape[0], stride=sublanes), :] = val[:, c * LANES:(c + 1) * LANES]


def _dispatch_kernel(dest_ref, zero_ref, x_ref, meta_ref, buf_hbm, row_scr, sem):
    tm = x_ref.shape[0]
    ts = TOKEN_SUBLANES
    i = pl.program_id(0)
    n = pl.num_programs(0)
    slot = i % 2

    @pl.when(i == 0)
    def _():
        row_scr[...] = jnp.zeros_like(row_scr)

        def zero_block(b, start):
            @pl.when(zero_ref[b] != 0)
            def _():
                for k in range(MOE_ROWS // tm):
                    rows = pl.ds((b * MOE_ROWS + k * tm) * ts, tm * ts)
                    cp = pltpu.make_async_copy(row_scr.at[1], buf_hbm.at[rows], sem.at[1])
                    if start:
                        cp.start()
                    else:
                        cp.wait()
        for b in range(zero_ref.shape[0]):
            zero_block(b, True)
        for b in range(zero_ref.shape[0]):
            zero_block(b, False)

    stage = row_scr.at[slot]

    @pl.when(i >= 2)
    def _():
        _tile_rows_wait(stage, buf_hbm, tm, sem.at[slot], ts)

    _store_tile_rows(stage, x_ref[...], ts)
    stage[pl.ds(ROW_SUBLANES, tm, stride=ts), :] = meta_ref[...]

    def issue(r, _):
        _tile_row_copy(stage, r, buf_hbm, dest_ref[i * tm + r], sem.at[slot], ts).start()
        return 0
    lax.fori_loop(0, tm, issue, 0, unroll=ISSUE_UNROLL)

    @pl.when(i == n - 1)
    def _():
        _tile_rows_wait(stage, buf_hbm, tm, sem.at[slot], ts)

        @pl.when(n >= 2)
        def _():
            _tile_rows_wait(row_scr.at[1 - slot], buf_hbm, tm, sem.at[1 - slot], ts)


def _dispatch(dest, zero_flag, x, meta, n_rows):
    T = x.shape[0]
    return pl.pallas_call(
        _dispatch_kernel,
        grid_spec=pltpu.PrefetchScalarGridSpec(
            num_scalar_prefetch=2,
            grid=(T // COMBINE_TILE,),
            in_specs=[pl.BlockSpec((COMBINE_TILE, D_MODEL), lambda i, d, z: (i, 0)),
                      pl.BlockSpec((COMBINE_TILE, LANES), lambda i, d, z: (i, 0))],
            out_specs=pl.BlockSpec(memory_space=pl.ANY),
            scratch_shapes=[pltpu.VMEM((2, COMBINE_TILE * TOKEN_SUBLANES, LANES), F32),
                            pltpu.SemaphoreType.DMA((2,))]),
        out_shape=jax.ShapeDtypeStruct((n_rows * TOKEN_SUBLANES, LANES), F32),
        compiler_params=pltpu.CompilerParams(dimension_semantics=("arbitrary",),
                                             has_side_effects=True),
        name="moe_dispatch",
    )(dest, zero_flag, x, meta)

def _ffn_kernel(bg_ref, nused_ref, nxt_ref, rows_ref, w1_hbm, w3_hbm, w2_hbm, o_ref,
                w1_f32, w3_f32, w2_f32, w1_ref, w3_ref, w2_ref, sem, *, layer):
    i = pl.program_id(0)
    used = i < nused_ref[0]
    epg = MOE_EXPERTS_PER_GROUP

    def fetch(group):
        first = pl.multiple_of(group * epg, epg)
        return [pltpu.make_async_copy(src.at[layer, pl.ds(first, epg)], dst, sem.at[k])
                for k, (src, dst) in enumerate(((w1_hbm, w1_f32), (w3_hbm, w3_f32),
                                                (w2_hbm, w2_f32)))]

    @pl.when(i == 0)
    def _():
        for cp in fetch(bg_ref[0]):
            cp.start()

    @pl.when(used & ((i == 0) | (bg_ref[i] != bg_ref[jnp.maximum(i - 1, 0)])))
    def _():
        for cp in fetch(bg_ref[i]):
            cp.wait()
        for e in range(epg):
            w1_ref[e] = w1_f32[e].astype(BF16)
            w3_ref[e] = w3_f32[e].astype(BF16)
            w2_ref[e] = w2_f32[e].astype(BF16)

        @pl.when(nxt_ref[i] >= 0)
        def _():
            for cp in fetch(nxt_ref[i]):
                cp.start()

    @pl.when(used)
    def _():
        x = _load_tile_rows(rows_ref, MOE_ROWS, TOKEN_SUBLANES).astype(BF16)
        meta = rows_ref[pl.ds(ROW_SUBLANES, MOE_ROWS, stride=TOKEN_SUBLANES), :]
        e1 = meta[:, META_E1:META_E1 + 1]
        e2 = meta[:, META_E2:META_E2 + 1]
        g1 = meta[:, META_G1:META_G1 + 1]
        g2 = meta[:, META_G2:META_G2 + 1]
        acc = None
        for e in range(MOE_EXPERTS_PER_GROUP):
            h1 = _dot(x, w1_ref[e])
            h3 = _dot(x, w3_ref[e])
            gate = jnp.where(e1 == e, g1, g2)
            h = h1 * (1.0 / (1.0 + jnp.exp(-h1))) * h3 * gate
            h = jnp.where((e1 == e) | (e2 == e), h, 0.0).astype(BF16)
            y = _dot(h, w2_ref[e])
            acc = y if acc is None else acc + y
        _store_tile_rows(o_ref, acc, ROW_SUBLANES)

    @pl.when(jnp.logical_not(used))
    def _():
        o_ref[...] = jnp.zeros_like(o_ref)


def _expert_ffn(blk_group, n_used, next_group, buf, w1, w3, w2, layer):
    n_rows = buf.shape[0] // TOKEN_SUBLANES
    epg = MOE_EXPERTS_PER_GROUP
    up = (epg, D_MODEL, MOE_D_EXPERT)
    down = (epg, MOE_D_EXPERT, D_MODEL)
    return pl.pallas_call(
        functools.partial(_ffn_kernel, layer=layer),
        grid_spec=pltpu.PrefetchScalarGridSpec(
            num_scalar_prefetch=3,
            grid=(n_rows // MOE_ROWS,),
            in_specs=[pl.BlockSpec((MOE_ROWS * TOKEN_SUBLANES, LANES),
                                   lambda i, bg, nu, nx: (jnp.minimum(i, nu[0] - 1), 0)),
                      pl.BlockSpec(memory_space=pl.ANY),
                      pl.BlockSpec(memory_space=pl.ANY),
                      pl.BlockSpec(memory_space=pl.ANY)],
            out_specs=pl.BlockSpec((MOE_ROWS * ROW_SUBLANES, LANES),
                                   lambda i, bg, nu, nx: (i, 0)),
            scratch_shapes=[pltpu.VMEM(up, F32), pltpu.VMEM(up, F32), pltpu.VMEM(down, F32),
                            pltpu.VMEM(up, BF16), pltpu.VMEM(up, BF16), pltpu.VMEM(down, BF16),
                            pltpu.SemaphoreType.DMA((3,))]),
        out_shape=jax.ShapeDtypeStruct((n_rows * ROW_SUBLANES, LANES), F32),
        compiler_params=_params(("arbitrary",), vmem=FFN_VMEM_LIMIT),
        name="moe_ffn",
    )(blk_group, n_used, next_group, buf, w1, w3, w2)


def _combine_kernel(dest_ref, y_hbm, x_ref, g_ref, b_ref, o_ref, y_scr, sem):
    tm = x_ref.shape[0]
    i = pl.program_id(0)

    def gather(tile, slot):
        def issue(r, _):
            _tile_row_copy(y_hbm, dest_ref[tile * tm + r], y_scr.at[slot], r, sem.at[slot],
                           ROW_SUBLANES).start()
            return 0
        lax.fori_loop(0, tm, issue, 0, unroll=ISSUE_UNROLL)

    @pl.when(i == 0)
    def _():
        gather(0, 0)

    @pl.when(i + 1 < pl.num_programs(0))
    def _():
        gather(i + 1, (i + 1) % 2)

    slot = i % 2
    _tile_rows_wait(y_hbm, y_scr.at[slot], tm, sem.at[slot], ROW_SUBLANES)
    y = DEEPNORM_ALPHA * x_ref[...] + _load_tile_rows(y_scr.at[slot], tm, ROW_SUBLANES)
    o_ref[...] = _layer_norm(y, g_ref[...], b_ref[...])


def _combine_ln(dest, y_buf, x, g, b):
    T = x.shape[0]
    row = lambda i, d: (i, 0)
    fixed = lambda i, d: (0, 0)
    return pl.pallas_call(
        _combine_kernel,
        grid_spec=pltpu.PrefetchScalarGridSpec(
            num_scalar_prefetch=1,
            grid=(T // COMBINE_TILE,),
            in_specs=[pl.BlockSpec(memory_space=pl.ANY),
                      pl.BlockSpec((COMBINE_TILE, D_MODEL), row),
                      pl.BlockSpec((1, D_MODEL), fixed),
                      pl.BlockSpec((1, D_MODEL), fixed)],
            out_specs=pl.BlockSpec((COMBINE_TILE, D_MODEL), row),
            scratch_shapes=[pltpu.VMEM((2, COMBINE_TILE * ROW_SUBLANES, LANES), F32),
                            pltpu.SemaphoreType.DMA((2,))]),
        out_shape=jax.ShapeDtypeStruct((T, D_MODEL), F32),
        compiler_params=_params(("arbitrary",)),
        name="moe_combine_ln",
    )(dest, y_buf, x, g.reshape(1, -1), b.reshape(1, -1))


def _hier_moe_ln(x, w_grp, b_grp, w_rt, b_rt, w1, w3, w2, layer, g2, b2):
    T = x.shape[0]
    pad_w = LANES - MOE_GROUPS - MOE_EXPERTS
    w_pad = jnp.concatenate([w_grp, w_rt, jnp.zeros((D_MODEL, pad_w), F32)], axis=1)
    b_pad = jnp.concatenate([b_grp, b_rt, jnp.zeros((pad_w,), F32)]).reshape(1, LANES)
    ids, meta, cnt = _router(x, w_pad, b_pad)

    counts = cnt[0, ROUTER_GROUP_LANE0:ROUTER_GROUP_LANE0 + MOE_GROUPS].astype(jnp.int32)
    padded = ((counts + MOE_ROWS - 1) // MOE_ROWS) * MOE_ROWS
    pad_end = jnp.cumsum(padded)
    pad_start = pad_end - padded
    dest = pad_start[ids[:, 0]] + ids[:, 1]
    n_blk = T // MOE_ROWS + MOE_GROUPS
    n_used = (pad_end[-1:] // MOE_ROWS).astype(jnp.int32)
    blk_row0 = jnp.minimum(jnp.arange(n_blk), n_used - 1) * MOE_ROWS
    blk_group = jnp.sum(pad_end[None, :] <= blk_row0[:, None], axis=1).astype(jnp.int32)
    blk_end = (jnp.arange(n_blk) + 1) * MOE_ROWS
    group_last = jnp.any((blk_end[:, None] == pad_end[None, :]) & (padded[None, :] > 0), axis=1)
    zero_flag = (group_last | (jnp.arange(n_blk) >= n_used)).astype(jnp.int32)

    after = pad_end[blk_group] // MOE_ROWS
    next_group = jnp.where(after < n_used, blk_group[jnp.minimum(after, n_blk - 1)], -1)
    next_group = next_group.astype(jnp.int32)

    buf = _dispatch(dest, zero_flag, x, meta, n_blk * MOE_ROWS)
    y_buf = _expert_ffn(blk_group, n_used, next_group, buf, w1, w3, w2, layer)
    return _combine_ln(dest, y_buf, x, g2, b2)


def kernel(x, moba_w_qkv, moba_w_o, gmlp_w_in, gmlp_b_in, gmlp_ln_g, gmlp_ln_b, gmlp_w_s,
           gmlp_b_s, gmlp_w_out, ln1_g, ln1_b, ln2_g, ln2_b, moe_w_grp, moe_b_grp, moe_w_rt,
           moe_b_rt, moe_w1, moe_w3, moe_w2):
    B, S, D = x.shape
    assert D == D_MODEL and S % MOBA_BLOCK == 0 and (B * S) % ROW_TILE == 0
    xf = x.reshape(B * S, D)
    for i in range(DEPTH):
        j = i // 2
        if i % 2 == 0:
            wqk = moba_w_qkv[j, :, :2 * D].astype(BF16)
            wvt = moba_w_qkv[j, :, 2 * D:].T.astype(BF16)
            qk, vt = _qkv_proj(xf, wqk, wvt)
            att = _moba_attention(qk, vt, B, S)
            xf = _proj_ln(att, moba_w_o[j].astype(BF16), xf, ln1_g[i], ln1_b[i])
        else:
            xf = _gmlp_layer(xf, gmlp_w_in[j].astype(BF16), gmlp_b_in[j], gmlp_ln_g[j],
                             gmlp_ln_b[j], gmlp_w_s[j], gmlp_b_s[j].T,
                             gmlp_w_out[j].astype(BF16), ln1_g[i], ln1_b[i])
        xf = _hier_moe_ln(xf, moe_w_grp[i], moe_b_grp[i], moe_w_rt[i], moe_b_rt[i],
                          moe_w1, moe_w3, moe_w2, i, ln2_g[i], ln2_b[i])
    return xf.reshape(B, S, D)
```

```python
import functools

import jax
import jax.numpy as jnp
from jax import lax
from jax.experimental import pallas as pl
from jax.experimental.pallas import tpu as pltpu

F32 = jnp.float32
BF16 = jnp.bfloat16

D_MODEL = 1024
DEPTH = 4
MOBA_HEADS = 16
MOBA_HEAD_DIM = 64
MOBA_BLOCK = 256
MOBA_TOPK = 3
GMLP_DV = 3072
GMLP_GROUPS = 8
GMLP_GROUP_DIM = GMLP_DV // GMLP_GROUPS
GMLP_CHUNK = 128
MOE_GROUPS = 8
MOE_EXPERTS_PER_GROUP = 8
MOE_EXPERTS = 64
MOE_D_EXPERT = 256
LN_EPS = 1e-5
DEEPNORM_ALPHA = (2.0 * DEPTH) ** 0.25

LANES = 128
BF16_SUBLANES = 16
ROW_SUBLANES = D_MODEL // LANES
TOKEN_SUBLANES = 2 * ROW_SUBLANES
LOG2_E = 1.4426950408889634
ROUTER_GROUP_LANE0 = 0
ROUTER_EXPERT_LANE0 = 8
MOE_ROWS = 512
META_E1, META_E2, META_G1, META_G2 = 0, 1, 2, 3
ROW_TILE = 512
COMBINE_TILE = 512
ISSUE_UNROLL = 8
VMEM_LIMIT = 56 * 1024 * 1024
FFN_VMEM_LIMIT = 62 * 1024 * 1024
NEG_INF = float("-inf")


def _params(sem, vmem=VMEM_LIMIT):
    return pltpu.CompilerParams(dimension_semantics=sem, vmem_limit_bytes=vmem)


def _layer_norm(y, g, b):
    mu = jnp.mean(y, axis=-1, keepdims=True)
    yc = y - mu
    var = jnp.mean(yc * yc, axis=-1, keepdims=True)
    return yc * lax.rsqrt(var + LN_EPS) * g + b


def _dot(a, b):
    return jnp.dot(a, b, preferred_element_type=F32)


def _dot_nt(a, b):
    return lax.dot_general(a, b, (((1,), (1,)), ((), ())), preferred_element_type=F32)


def _split_bf16(x):
    hi = x.astype(BF16)
    lo = (x - hi.astype(F32)).astype(BF16)
    return hi, lo


def _qkv_kernel(x_ref, wk_ref, wqvt_ref, k_ref, qvt_ref):
    xb = x_ref[...].astype(BF16)
    k_ref[...] = _dot(xb, wk_ref[...]).astype(BF16)
    qt = _dot_nt(wqvt_ref[:D_MODEL, :], xb)
    qvt_ref[:D_MODEL, :] = (qt * (MOBA_HEAD_DIM ** -0.5 * LOG2_E)).astype(BF16)
    qvt_ref[D_MODEL:, :] = _dot_nt(wqvt_ref[D_MODEL:, :], xb).astype(BF16)


def _qkv_proj(x, wk_bf16, wqvt_bf16):
    T = x.shape[0]
    return pl.pallas_call(
        _qkv_kernel,
        grid=(T // ROW_TILE,),
        in_specs=[pl.BlockSpec((ROW_TILE, D_MODEL), lambda i: (i, 0)),
                  pl.BlockSpec((D_MODEL, D_MODEL), lambda i: (0, 0)),
                  pl.BlockSpec((2 * D_MODEL, D_MODEL), lambda i: (0, 0))],
        out_specs=[pl.BlockSpec((ROW_TILE, D_MODEL), lambda i: (i, 0)),
                   pl.BlockSpec((2 * D_MODEL, ROW_TILE), lambda i: (0, i))],
        out_shape=[jax.ShapeDtypeStruct((T, D_MODEL), BF16),
                   jax.ShapeDtypeStruct((2 * D_MODEL, T), BF16)],
        compiler_params=_params(("parallel",)),
        name="moba_qkv",
    )(x, wk_bf16, wqvt_bf16)


def _moba_kernel(qt_ref, k_ref, vt_ref, o_ref, s_scr, p_scr, vt_scr, *, n_blocks):
    KB = MOBA_BLOCK
    S = n_blocks * KB
    hd = MOBA_HEAD_DIM
    for h in range(2):
        vt_scr[h, :hd, :] = vt_ref[h * hd:(h + 1) * hd, :]
        vt_scr[h, hd:, :] = jnp.ones((vt_scr.shape[1] - hd, S), BF16)
    qt = qt_ref[...]
    drow = lax.broadcasted_iota(jnp.int32, (LANES, S), 0)
    blk = lax.broadcasted_iota(jnp.int32, (n_blocks, S), 0)
    cur = jnp.right_shift(lax.broadcasted_iota(jnp.int32, (n_blocks, S), 1),
                          KB.bit_length() - 1)
    key_i = lax.broadcasted_iota(jnp.int32, (KB, KB), 0)
    qry_i = lax.broadcasted_iota(jnp.int32, (KB, KB), 1)

    kf = k_ref[...].astype(F32).reshape(n_blocks, KB, LANES)
    kmean = jnp.sum(kf, axis=1) * (1.0 / KB)
    km_hi, km_lo = _split_bf16(kmean)

    qh, bias = [], []
    for h in range(2):
        head_rows = (drow < MOBA_HEAD_DIM) if h == 0 else (drow >= MOBA_HEAD_DIM)
        qh_h = jnp.where(head_rows, qt, jnp.zeros_like(qt))
        bs = _dot(km_hi, qh_h) + _dot(km_lo, qh_h)
        cnt = jnp.zeros((n_blocks, S), jnp.int32)
        for jp in range(n_blocks):
            cj = bs[jp:jp + 1, :]
            beats = (cj > bs) | ((cj == bs) & (jp < blk))
            cnt = cnt + jnp.where(beats & (jp < cur), 1, 0)
        qh.append(qh_h)
        bias.append(jnp.where((blk < cur) & (cnt < MOBA_TOPK), 0.0, NEG_INF))

    steps = [(i, h) for i in range(n_blocks) for h in range(2)]
    n_slots = s_scr.shape[0]
    state = [dict(m=None, acc=None) for _ in steps]

    def block_bias(n, j):
        i, h = steps[n]
        return bias[h][j:j + 1, i * KB:(i + 1) * KB]

    def scores(n):
        i, h = steps[n]
        qi = qh[h][:, i * KB:(i + 1) * KB]
        state[n]["s"] = _dot(k_ref[0:(i + 1) * KB, :], qi)

    def pass1_block(n, j):
        i, _ = steps[n]
        st = state[n]
        s = st["s"][j * KB:(j + 1) * KB, :]
        if j == i:
            s = jnp.where(key_i <= qry_i, s, NEG_INF)
        s_scr[n % n_slots, j * KB:(j + 1) * KB, :] = s
        cm = jnp.max(s, axis=0, keepdims=True)
        if j < i:
            cm = cm + block_bias(n, j)
        st["m"] = cm if st["m"] is None else jnp.maximum(st["m"], cm)

    def pass2_block(n, j):
        i, _ = steps[n]
        m = state[n]["m"]
        shift = -m if j == i else block_bias(n, j) - m
        p = jnp.exp2(s_scr[n % n_slots, j * KB:(j + 1) * KB, :] + shift)
        p_scr[n % n_slots, j * KB:(j + 1) * KB, :] = p.astype(BF16)

    def finish(n):
        i, h = steps[n]
        state[n]["acc"] = _dot(vt_scr[h, :, 0:(i + 1) * KB],
                               p_scr[n % n_slots, 0:(i + 1) * KB, :])
        state[n]["s"] = None
        if h == 1:
            a0, a1 = state[n - 1]["acc"], state[n]["acc"]
            ot = jnp.concatenate([a0[:hd] / a0[hd:hd + 1], a1[:hd] / a1[hd:hd + 1]], axis=0)
            o_ref[i * KB:(i + 1) * KB, :] = ot.T.astype(BF16)

    scores(0)
    for j in range(steps[0][0] + 1):
        pass1_block(0, j)
    for n in range(len(steps)):
        cur_blocks = steps[n][0] + 1
        nxt_blocks = steps[n + 1][0] + 1 if n + 1 < len(steps) else 0
        if nxt_blocks:
            scores(n + 1)
        for j in range(max(cur_blocks, nxt_blocks)):
            if j < nxt_blocks:
                pass1_block(n + 1, j)
            if j < cur_blocks:
                pass2_block(n, j)
        finish(n)


def _moba_attention(k, qvt, batch, seq):
    T = batch * seq
    nb = seq // MOBA_BLOCK
    n_pairs = MOBA_HEADS // 2
    return pl.pallas_call(
        functools.partial(_moba_kernel, n_blocks=nb),
        grid=(batch, n_pairs),
        in_specs=[pl.BlockSpec((LANES, seq), lambda b, hp: (hp, b)),
                  pl.BlockSpec((seq, LANES), lambda b, hp: (b, hp)),
                  pl.BlockSpec((LANES, seq), lambda b, hp: (n_pairs + hp, b))],
        out_specs=pl.BlockSpec((seq, LANES), lambda b, hp: (b, hp)),
        out_shape=jax.ShapeDtypeStruct((T, D_MODEL), BF16),
        scratch_shapes=[pltpu.VMEM((4, seq, MOBA_BLOCK), F32),
                        pltpu.VMEM((4, seq, MOBA_BLOCK), BF16),
                        pltpu.VMEM((2, MOBA_HEAD_DIM + BF16_SUBLANES, seq), BF16)],
        compiler_params=_params(("parallel", "parallel")),
        name="moba_attn",
    )(qvt, k, qvt)


def _proj_ln_kernel(a_ref, w_ref, x_ref, g_ref, b_ref, o_ref):
    y = DEEPNORM_ALPHA * x_ref[...] + _dot(a_ref[...], w_ref[...])
    o_ref[...] = _layer_norm(y, g_ref[...], b_ref[...])


def _proj_ln(a, w_bf16, x, g, b):
    T = x.shape[0]
    row = lambda i: (i, 0)
    fixed = lambda i: (0, 0)
    return pl.pallas_call(
        _proj_ln_kernel,
        grid=(T // ROW_TILE,),
        in_specs=[pl.BlockSpec((ROW_TILE, D_MODEL), row),
                  pl.BlockSpec((D_MODEL, D_MODEL), fixed),
                  pl.BlockSpec((ROW_TILE, D_MODEL), row),
                  pl.BlockSpec((1, D_MODEL), fixed),
                  pl.BlockSpec((1, D_MODEL), fixed)],
        out_specs=pl.BlockSpec((ROW_TILE, D_MODEL), row),
        out_shape=jax.ShapeDtypeStruct((T, D_MODEL), F32),
        compiler_params=_params(("parallel",)),
        name="moba_out_ln",
    )(a, w_bf16, x, g.reshape(1, -1), b.reshape(1, -1))


def _gelu(z):
    return 0.5 * z * (1.0 + lax.erf(z * (2.0 ** -0.5)))


def _gmlp_kernel(x_ref, win_ref, bin_ref, lg_ref, lb_ref, ws_ref, bs_ref, wout_ref,
                 g1_ref, b1_ref, o_ref, u_scr, v_scr, gate_scr):
    tm = x_ref.shape[0]
    cw = GMLP_DV // 4
    xb = x_ref[...].astype(BF16)
    for c in range(8):
        z = _dot(xb, win_ref[:, c * cw:(c + 1) * cw]) + bin_ref[:, c * cw:(c + 1) * cw]
        z = _gelu(z)
        if c < 4:
            u_scr[:, c * cw:(c + 1) * cw] = z.astype(BF16)
        else:
            v_scr[:, (c - 4) * cw:(c - 3) * cw] = z
    vn = _layer_norm(v_scr[...], lg_ref[...], lb_ref[...]).astype(BF16)

    row = lax.broadcasted_iota(jnp.int32, (GMLP_CHUNK, GMLP_CHUNK), 0)
    col = lax.broadcasted_iota(jnp.int32, (GMLP_CHUNK, GMLP_CHUNK), 1)
    gd = GMLP_GROUP_DIM
    for g in range(GMLP_GROUPS):
        wc = jnp.where(col <= row, ws_ref[g], 0.0).astype(BF16)
        bias = bs_ref[:, g:g + 1]
        for c in range(tm // GMLP_CHUNK):
            r0 = c * GMLP_CHUNK
            mixed = _dot(wc, vn[r0:r0 + GMLP_CHUNK, g * gd:(g + 1) * gd]) + bias
            u = u_scr[r0:r0 + GMLP_CHUNK, g * gd:(g + 1) * gd].astype(F32)
            gate_scr[r0:r0 + GMLP_CHUNK, g * gd:(g + 1) * gd] = (u * mixed).astype(BF16)
    y = DEEPNORM_ALPHA * x_ref[...] + _dot(gate_scr[...], wout_ref[...])
    o_ref[...] = _layer_norm(y, g1_ref[...], b1_ref[...])


def _gmlp_layer(x, w_in, b_in, ln_g, ln_b, w_s, b_s_t, w_out, g1, b1):
    T = x.shape[0]
    row = lambda i: (i, 0)
    fixed = lambda i: (0, 0)
    once = pl.Buffered(1)
    return pl.pallas_call(
        _gmlp_kernel,
        grid=(T // ROW_TILE,),
        in_specs=[pl.BlockSpec((ROW_TILE, D_MODEL), row),
                  pl.BlockSpec((D_MODEL, 2 * GMLP_DV), fixed, pipeline_mode=once),
                  pl.BlockSpec((1, 2 * GMLP_DV), fixed),
                  pl.BlockSpec((1, GMLP_DV), fixed),
                  pl.BlockSpec((1, GMLP_DV), fixed),
                  pl.BlockSpec((GMLP_GROUPS, GMLP_CHUNK, GMLP_CHUNK), lambda i: (0, 0, 0)),
                  pl.BlockSpec((GMLP_CHUNK, GMLP_GROUPS), fixed),
                  pl.BlockSpec((GMLP_DV, D_MODEL), fixed, pipeline_mode=once),
                  pl.BlockSpec((1, D_MODEL), fixed),
                  pl.BlockSpec((1, D_MODEL), fixed)],
        out_specs=pl.BlockSpec((ROW_TILE, D_MODEL), row),
        out_shape=jax.ShapeDtypeStruct((T, D_MODEL), F32),
        scratch_shapes=[pltpu.VMEM((ROW_TILE, GMLP_DV), BF16),
                        pltpu.VMEM((ROW_TILE, GMLP_DV), F32),
                        pltpu.VMEM((ROW_TILE, GMLP_DV), BF16)],
        compiler_params=_params(("parallel",)),
        name="gmlp",
    )(x, w_in, b_in.reshape(1, -1), ln_g.reshape(1, -1), ln_b.reshape(1, -1), w_s, b_s_t,
      w_out, g1.reshape(1, -1), b1.reshape(1, -1))


def _router_kernel(x_ref, w_ref, b_ref, ids_ref, meta_ref, cnt_ref, carry):
    tm = x_ref.shape[0]

    @pl.when(pl.program_id(0) == 0)
    def _():
        carry[...] = jnp.zeros_like(carry)

    xh, xl = _split_bf16(x_ref[...])
    wh, wl = _split_bf16(w_ref[...])
    hi = _dot(xh, jnp.concatenate([wh, wl], axis=1))
    logits = (hi[:, :LANES] + hi[:, LANES:]) + _dot(xl, wh) + b_ref[...]
    lane_i = lax.broadcasted_iota(jnp.int32, (tm, LANES), 1)
    lane = lane_i.astype(F32)

    def first_max(vals):
        top = jnp.max(vals, axis=-1, keepdims=True)
        idx = jnp.min(jnp.where(vals == top, lane, float(LANES)), axis=-1, keepdims=True)
        return top, idx

    gl = jnp.where(lane_i < ROUTER_EXPERT_LANE0, logits, NEG_INF)
    g_top, g_idx = first_max(gl)
    g_p = 1.0 / jnp.sum(jnp.exp(gl - g_top), axis=-1, keepdims=True)
    lo = ROUTER_EXPERT_LANE0 + g_idx * MOE_EXPERTS_PER_GROUP
    el = jnp.where((lane >= lo) & (lane < lo + MOE_EXPERTS_PER_GROUP), logits, NEG_INF)
    e1, i1 = first_max(el)
    e2, i2 = first_max(jnp.where(lane == i1, NEG_INF, el))
    d = jnp.exp(e2 - e1)
    gate1 = g_p / (1.0 + d)
    gate2 = g_p * d / (1.0 + d)

    r = lax.broadcasted_iota(jnp.int32, (tm, tm), 0)
    c = lax.broadcasted_iota(jnp.int32, (tm, tm), 1)
    before = jnp.where(c < r, 1.0, 0.0).astype(BF16)
    oh = lane == g_idx
    ohf = jnp.where(oh, 1.0, 0.0)
    ahead = _dot(before, ohf.astype(BF16))
    base = carry[...]
    rank = jnp.sum(jnp.where(oh, base + ahead, 0.0), axis=-1, keepdims=True)
    total = base + jnp.sum(ohf, axis=0, keepdims=True)
    carry[...] = total
    cnt_ref[...] = jnp.broadcast_to(total, cnt_ref.shape)

    ids_ref[...] = jnp.where(lane_i == 0, g_idx.astype(jnp.int32),
                   jnp.where(lane_i == 1, rank.astype(jnp.int32), 0))
    meta_ref[...] = jnp.where(lane_i == META_E1, i1 - lo,
                    jnp.where(lane_i == META_E2, i2 - lo,
                    jnp.where(lane_i == META_G1, gate1,
                    jnp.where(lane_i == META_G2, gate2, 0.0))))


def _router(x, w_pad, b_pad):
    T = x.shape[0]
    row = lambda i: (i, 0)
    fixed = lambda i: (0, 0)
    return pl.pallas_call(
        _router_kernel,
        grid=(T // ROW_TILE,),
        in_specs=[pl.BlockSpec((ROW_TILE, D_MODEL), row),
                  pl.BlockSpec((D_MODEL, LANES), fixed),
                  pl.BlockSpec((1, LANES), fixed)],
        out_specs=[pl.BlockSpec((ROW_TILE, LANES), row),
                   pl.BlockSpec((ROW_TILE, LANES), row),
                   pl.BlockSpec((8, LANES), fixed)],
        out_shape=[jax.ShapeDtypeStruct((T, LANES), jnp.int32),
                   jax.ShapeDtypeStruct((T, LANES), F32),
                   jax.ShapeDtypeStruct((8, LANES), F32)],
        scratch_shapes=[pltpu.VMEM((1, LANES), F32)],
        compiler_params=_params(("arbitrary",)),
        name="moe_router",
    )(x, w_pad, b_pad)


def _tile_row(ref, row, sublanes):
    return ref.at[pl.ds(pl.multiple_of(row * sublanes, sublanes), sublanes)]


def _tile_row_copy(src_ref, src_row, dst_ref, dst_row, sem, sublanes):
    return pltpu.make_async_copy(_tile_row(src_ref, src_row, sublanes),
                                 _tile_row(dst_ref, dst_row, sublanes), sem)


def _tile_rows_wait(src_ref, dst_ref, n_rows, sem, sublanes):
    n = n_rows * sublanes
    pltpu.make_async_copy(src_ref.at[pl.ds(0, n)], dst_ref.at[pl.ds(0, n)], sem).wait()


def _load_tile_rows(ref, n_rows, sublanes):
    return jnp.concatenate([ref[pl.ds(c, n_rows, stride=sublanes), :]
                            for c in range(ROW_SUBLANES)], axis=1)


def _store_tile_rows(ref, val, sublanes):
    for c in range(ROW_SUBLANES):
        ref[pl.ds(c, val.shape[0], stride=sublanes), :] = val[:, c * LANES:(c + 1) * LANES]


def _dispatch_kernel(dest_ref, zero_ref, x_ref, meta_ref, buf_hbm, row_scr, sem):
    tm = x_ref.shape[0]
    ts = TOKEN_SUBLANES
    i = pl.program_id(0)
    n = pl.num_programs(0)
    slot = i % 2

    @pl.when(i == 0)
    def _():
        row_scr[...] = jnp.zeros_like(row_scr)

        def zero_block(b, start):
            @pl.when(zero_ref[b] != 0)
            def _():
                for k in range(MOE_ROWS // tm):
                    rows = pl.ds((b * MOE_ROWS + k * tm) * ts, tm * ts)
                    cp = pltpu.make_async_copy(row_scr.at[1], buf_hbm.at[rows], sem.at[1])
                    if start:
                        cp.start()
                    else:
                        cp.wait()
        for b in range(zero_ref.shape[0]):
            zero_block(b, True)
        for b in range(zero_ref.shape[0]):
            zero_block(b, False)

    stage = row_scr.at[slot]

    @pl.when(i >= 2)
    def _():
        _tile_rows_wait(stage, buf_hbm, tm, sem.at[slot], ts)

    _store_tile_rows(stage, x_ref[...], ts)
    stage[pl.ds(ROW_SUBLANES, tm, stride=ts), :] = meta_ref[...]

    def issue(r, _):
        _tile_row_copy(stage, r, buf_hbm, dest_ref[i * tm + r], sem.at[slot], ts).start()
        return 0
    lax.fori_loop(0, tm, issue, 0, unroll=ISSUE_UNROLL)

    @pl.when(i == n - 1)
    def _():
        _tile_rows_wait(stage, buf_hbm, tm, sem.at[slot], ts)

        @pl.when(n >= 2)
        def _():
            _tile_rows_wait(row_scr.at[1 - slot], buf_hbm, tm, sem.at[1 - slot], ts)


def _dispatch(dest, zero_flag, x, meta, n_rows):
    T = x.shape[0]
    return pl.pallas_call(
        _dispatch_kernel,
        grid_spec=pltpu.PrefetchScalarGridSpec(
            num_scalar_prefetch=2,
            grid=(T // COMBINE_TILE,),
            in_specs=[pl.BlockSpec((COMBINE_TILE, D_MODEL), lambda i, d, z: (i, 0)),
                      pl.BlockSpec((COMBINE_TILE, LANES), lambda i, d, z: (i, 0))],
            out_specs=pl.BlockSpec(memory_space=pl.ANY),
            scratch_shapes=[pltpu.VMEM((2, COMBINE_TILE * TOKEN_SUBLANES, LANES), F32),
                            pltpu.SemaphoreType.DMA((2,))]),
        out_shape=jax.ShapeDtypeStruct((n_rows * TOKEN_SUBLANES, LANES), F32),
        compiler_params=pltpu.CompilerParams(dimension_semantics=("arbitrary",),
                                             has_side_effects=True),
        name="moe_dispatch",
    )(dest, zero_flag, x, meta)

def _ffn_kernel(bg_ref, nused_ref, nxt_ref, rows_ref, w1_hbm, w3_hbm, w2_hbm, o_ref,
                w1_f32, w3_f32, w2_f32, w1_ref, w3_ref, w2_ref, sem, *, layer):
    i = pl.program_id(0)
    used = i < nused_ref[0]
    epg = MOE_EXPERTS_PER_GROUP

    def fetch(group):
        first = pl.multiple_of(group * epg, epg)
        return [pltpu.make_async_copy(src.at[layer, pl.ds(first, epg)], dst, sem.at[k])
                for k, (src, dst) in enumerate(((w1_hbm, w1_f32), (w3_hbm, w3_f32),
                                                (w2_hbm, w2_f32)))]

    @pl.when(i == 0)
    def _():
        for cp in fetch(bg_ref[0]):
            cp.start()

    @pl.when(used & ((i == 0) | (bg_ref[i] != bg_ref[jnp.maximum(i - 1, 0)])))
    def _():
        for cp in fetch(bg_ref[i]):
            cp.wait()
        for e in range(epg):
            w1_ref[e] = w1_f32[e].astype(BF16)
            w3_ref[e] = w3_f32[e].astype(BF16)
            w2_ref[e] = w2_f32[e].astype(BF16)

        @pl.when(nxt_ref[i] >= 0)
        def _():
            for cp in fetch(nxt_ref[i]):
                cp.start()

    @pl.when(used)
    def _():
        x = _load_tile_rows(rows_ref, MOE_ROWS, TOKEN_SUBLANES).astype(BF16)
        meta = rows_ref[pl.ds(ROW_SUBLANES, MOE_ROWS, stride=TOKEN_SUBLANES), :]
        e1 = meta[:, META_E1:META_E1 + 1]
        e2 = meta[:, META_E2:META_E2 + 1]
        g1 = meta[:, META_G1:META_G1 + 1]
        g2 = meta[:, META_G2:META_G2 + 1]
        acc = None
        for e in range(MOE_EXPERTS_PER_GROUP):
            h1 = _dot(x, w1_ref[e])
            h3 = _dot(x, w3_ref[e])
            gate = jnp.where(e1 == e, g1, g2)
            h = h1 * (1.0 / (1.0 + jnp.exp(-h1))) * h3 * gate
            h = jnp.where((e1 == e) | (e2 == e), h, 0.0).astype(BF16)
            y = _dot(h, w2_ref[e])
            acc = y if acc is None else acc + y
        _store_tile_rows(o_ref, acc, ROW_SUBLANES)

    @pl.when(jnp.logical_not(used))
    def _():
        o_ref[...] = jnp.zeros_like(o_ref)


def _expert_ffn(blk_group, n_used, next_group, buf, w1, w3, w2, layer):
    n_rows = buf.shape[0] // TOKEN_SUBLANES
    epg = MOE_EXPERTS_PER_GROUP
    up = (epg, D_MODEL, MOE_D_EXPERT)
    down = (epg, MOE_D_EXPERT, D_MODEL)
    return pl.pallas_call(
        functools.partial(_ffn_kernel, layer=layer),
        grid_spec=pltpu.PrefetchScalarGridSpec(
            num_scalar_prefetch=3,
            grid=(n_rows // MOE_ROWS,),
            in_specs=[pl.BlockSpec((MOE_ROWS * TOKEN_SUBLANES, LANES),
                                   lambda i, bg, nu, nx: (jnp.minimum(i, nu[0] - 1), 0)),
                      pl.BlockSpec(memory_space=pl.ANY),
                      pl.BlockSpec(memory_space=pl.ANY),
                      pl.BlockSpec(memory_space=pl.ANY)],
            out_specs=pl.BlockSpec((MOE_ROWS * ROW_SUBLANES, LANES),
                                   lambda i, bg, nu, nx: (i, 0)),
            scratch_shapes=[pltpu.VMEM(up, F32), pltpu.VMEM(up, F32), pltpu.VMEM(down, F32),
                            pltpu.VMEM(up, BF16), pltpu.VMEM(up, BF16), pltpu.VMEM(down, BF16),
                            pltpu.SemaphoreType.DMA((3,))]),
        out_shape=jax.ShapeDtypeStruct((n_rows * ROW_SUBLANES, LANES), F32),
        compiler_params=_params(("arbitrary",), vmem=FFN_VMEM_LIMIT),
        name="moe_ffn",
    )(blk_group, n_used, next_group, buf, w1, w3, w2)


def _combine_kernel(dest_ref, y_hbm, x_ref, g_ref, b_ref, o_ref, y_scr, sem):
    tm = x_ref.shape[0]
    i = pl.program_id(0)

    def gather(tile, slot):
        def issue(r, _):
            _tile_row_copy(y_hbm, dest_ref[tile * tm + r], y_scr.at[slot], r, sem.at[slot],
                           ROW_SUBLANES).start()
            return 0
        lax.fori_loop(0, tm, issue, 0, unroll=ISSUE_UNROLL)

    @pl.when(i == 0)
    def _():
        gather(0, 0)

    @pl.when(i + 1 < pl.num_programs(0))
    def _():
        gather(i + 1, (i + 1) % 2)

    slot = i % 2
    _tile_rows_wait(y_hbm, y_scr.at[slot], tm, sem.at[slot], ROW_SUBLANES)
    y = DEEPNORM_ALPHA * x_ref[...] + _load_tile_rows(y_scr.at[slot], tm, ROW_SUBLANES)
    o_ref[...] = _layer_norm(y, g_ref[...], b_ref[...])


def _combine_ln(dest, y_buf, x, g, b):
    T = x.shape[0]
    row = lambda i, d: (i, 0)
    fixed = lambda i, d: (0, 0)
    return pl.pallas_call(
        _combine_kernel,
        grid_spec=pltpu.PrefetchScalarGridSpec(
            num_scalar_prefetch=1,
            grid=(T // COMBINE_TILE,),
            in_specs=[pl.BlockSpec(memory_space=pl.ANY),
                      pl.BlockSpec((COMBINE_TILE, D_MODEL), row),
                      pl.BlockSpec((1, D_MODEL), fixed),
                      pl.BlockSpec((1, D_MODEL), fixed)],
            out_specs=pl.BlockSpec((COMBINE_TILE, D_MODEL), row),
            scratch_shapes=[pltpu.VMEM((2, COMBINE_TILE * ROW_SUBLANES, LANES), F32),
                            pltpu.SemaphoreType.DMA((2,))]),
        out_shape=jax.ShapeDtypeStruct((T, D_MODEL), F32),
        compiler_params=_params(("arbitrary",)),
        name="moe_combine_ln",
    )(dest, y_buf, x, g.reshape(1, -1), b.reshape(1, -1))


def _hier_moe_ln(x, w_grp, b_grp, w_rt, b_rt, w1, w3, w2, layer, g2, b2):
    T = x.shape[0]
    pad_w = LANES - MOE_GROUPS - MOE_EXPERTS
    w_pad = jnp.concatenate([w_grp, w_rt, jnp.zeros((D_MODEL, pad_w), F32)], axis=1)
    b_pad = jnp.concatenate([b_grp, b_rt, jnp.zeros((pad_w,), F32)]).reshape(1, LANES)
    ids, meta, cnt = _router(x, w_pad, b_pad)

    counts = cnt[0, ROUTER_GROUP_LANE0:ROUTER_GROUP_LANE0 + MOE_GROUPS].astype(jnp.int32)
    padded = ((counts + MOE_ROWS - 1) // MOE_ROWS) * MOE_ROWS
    pad_end = jnp.cumsum(padded)
    pad_start = pad_end - padded
    dest = pad_start[ids[:, 0]] + ids[:, 1]
    n_blk = T // MOE_ROWS + MOE_GROUPS
    n_used = (pad_end[-1:] // MOE_ROWS).astype(jnp.int32)
    blk_row0 = jnp.minimum(jnp.arange(n_blk), n_used - 1) * MOE_ROWS
    blk_group = jnp.sum(pad_end[None, :] <= blk_row0[:, None], axis=1).astype(jnp.int32)
    blk_end = (jnp.arange(n_blk) + 1) * MOE_ROWS
    group_last = jnp.any((blk_end[:, None] == pad_end[None, :]) & (padded[None, :] > 0), axis=1)
    zero_flag = (group_last | (jnp.arange(n_blk) >= n_used)).astype(jnp.int32)

    after = pad_end[blk_group] // MOE_ROWS
    next_group = jnp.where(after < n_used, blk_group[jnp.minimum(after, n_blk - 1)], -1)
    next_group = next_group.astype(jnp.int32)

    buf = _dispatch(dest, zero_flag, x, meta, n_blk * MOE_ROWS)
    y_buf = _expert_ffn(blk_group, n_used, next_group, buf, w1, w3, w2, layer)
    return _combine_ln(dest, y_buf, x, g2, b2)


def kernel(x, moba_w_qkv, moba_w_o, gmlp_w_in, gmlp_b_in, gmlp_ln_g, gmlp_ln_b, gmlp_w_s,
           gmlp_b_s, gmlp_w_out, ln1_g, ln1_b, ln2_g, ln2_b, moe_w_grp, moe_b_grp, moe_w_rt,
           moe_b_rt, moe_w1, moe_w3, moe_w2):
    B, S, D = x.shape
    assert D == D_MODEL and S % MOBA_BLOCK == 0 and (B * S) % ROW_TILE == 0
    xf = x.reshape(B * S, D)
    for i in range(DEPTH):
        j = i // 2
        if i % 2 == 0:
            wk = moba_w_qkv[j, :, D:2 * D].astype(BF16)
            wqvt = jnp.concatenate([moba_w_qkv[j, :, :D], moba_w_qkv[j, :, 2 * D:]],
                                   axis=1).T.astype(BF16)
            k, qvt = _qkv_proj(xf, wk, wqvt)
            att = _moba_attention(k, qvt, B, S)
            xf = _proj_ln(att, moba_w_o[j].astype(BF16), xf, ln1_g[i], ln1_b[i])
        else:
            xf = _gmlp_layer(xf, gmlp_w_in[j].astype(BF16), gmlp_b_in[j], gmlp_ln_g[j],
                             gmlp_ln_b[j], gmlp_w_s[j], gmlp_b_s[j].T,
                             gmlp_w_out[j].astype(BF16), ln1_g[i], ln1_b[i])
        xf = _hier_moe_ln(xf, moe_w_grp[i], moe_b_grp[i], moe_w_rt[i], moe_b_rt[i],
                          moe_w1, moe_w3, moe_w2, i, ln2_g[i], ln2_b[i])
    return xf.reshape(B, S, D)
```

```python
import functools

import jax
import jax.numpy as jnp
from jax import lax
from jax.experimental import pallas as pl
from jax.experimental.pallas import tpu as pltpu

F32 = jnp.float32
BF16 = jnp.bfloat16

D_MODEL = 1024
DEPTH = 4
MOBA_HEADS = 16
MOBA_HEAD_DIM = 64
MOBA_BLOCK = 256
MOBA_TOPK = 3
GMLP_DV = 3072
GMLP_GROUPS = 8
GMLP_GROUP_DIM = GMLP_DV // GMLP_GROUPS
GMLP_CHUNK = 128
MOE_GROUPS = 8
MOE_EXPERTS_PER_GROUP = 8
MOE_EXPERTS = 64
MOE_D_EXPERT = 256
LN_EPS = 1e-5
DEEPNORM_ALPHA = (2.0 * DEPTH) ** 0.25

LANES = 128
BF16_SUBLANES = 16
ROW_SUBLANES = D_MODEL // LANES
TOKEN_SUBLANES = 2 * ROW_SUBLANES
LOG2_E = 1.4426950408889634
ROUTER_GROUP_LANE0 = 0
ROUTER_EXPERT_LANE0 = 8
ROUTER_ID_LANES = 8
MOE_ROWS = 512
META_E1, META_E2, META_G1, META_G2 = 0, 1, 2, 3
ROW_TILE = 512
COMBINE_TILE = 512
ISSUE_UNROLL = 8
VMEM_LIMIT = 56 * 1024 * 1024
FFN_VMEM_LIMIT = 62 * 1024 * 1024
NEG_INF = float("-inf")


def _params(sem, vmem=VMEM_LIMIT):
    return pltpu.CompilerParams(dimension_semantics=sem, vmem_limit_bytes=vmem)


def _layer_norm(y, g, b):
    mu = jnp.mean(y, axis=-1, keepdims=True)
    yc = y - mu
    var = jnp.mean(yc * yc, axis=-1, keepdims=True)
    return yc * lax.rsqrt(var + LN_EPS) * g + b


def _dot(a, b):
    return jnp.dot(a, b, preferred_element_type=F32)


def _dot_nt(a, b):
    return lax.dot_general(a, b, (((1,), (1,)), ((), ())), preferred_element_type=F32)


def _split_bf16(x):
    hi = x.astype(BF16)
    lo = (x - hi.astype(F32)).astype(BF16)
    return hi, lo


def _qkv_kernel(x_ref, wk_ref, wqvt_ref, k_ref, qvt_ref):
    xb = x_ref[...].astype(BF16)
    k_ref[...] = _dot(xb, wk_ref[...]).astype(BF16)
    qt = _dot_nt(wqvt_ref[:D_MODEL, :], xb)
    qvt_ref[:D_MODEL, :] = (qt * (MOBA_HEAD_DIM ** -0.5 * LOG2_E)).astype(BF16)
    qvt_ref[D_MODEL:, :] = _dot_nt(wqvt_ref[D_MODEL:, :], xb).astype(BF16)


def _qkv_proj(x, wk_bf16, wqvt_bf16):
    T = x.shape[0]
    return pl.pallas_call(
        _qkv_kernel,
        grid=(T // ROW_TILE,),
        in_specs=[pl.BlockSpec((ROW_TILE, D_MODEL), lambda i: (i, 0)),
                  pl.BlockSpec((D_MODEL, D_MODEL), lambda i: (0, 0)),
                  pl.BlockSpec((2 * D_MODEL, D_MODEL), lambda i: (0, 0))],
        out_specs=[pl.BlockSpec((ROW_TILE, D_MODEL), lambda i: (i, 0)),
                   pl.BlockSpec((2 * D_MODEL, ROW_TILE), lambda i: (0, i))],
        out_shape=[jax.ShapeDtypeStruct((T, D_MODEL), BF16),
                   jax.ShapeDtypeStruct((2 * D_MODEL, T), BF16)],
        compiler_params=_params(("parallel",)),
        name="moba_qkv",
    )(x, wk_bf16, wqvt_bf16)


def _moba_kernel(qt_ref, k_ref, vt_ref, o_ref, s_scr, p_scr, vt_scr, *, n_blocks):
    KB = MOBA_BLOCK
    S = n_blocks * KB
    hd = MOBA_HEAD_DIM
    for h in range(2):
        vt_scr[h, :hd, :] = vt_ref[h * hd:(h + 1) * hd, :]
        vt_scr[h, hd:, :] = jnp.ones((vt_scr.shape[1] - hd, S), BF16)
    qt = qt_ref[...]
    drow = lax.broadcasted_iota(jnp.int32, (LANES, S), 0)
    blk = lax.broadcasted_iota(jnp.int32, (n_blocks, S), 0)
    cur = jnp.right_shift(lax.broadcasted_iota(jnp.int32, (n_blocks, S), 1),
                          KB.bit_length() - 1)
    key_i = lax.broadcasted_iota(jnp.int32, (KB, KB), 0)
    qry_i = lax.broadcasted_iota(jnp.int32, (KB, KB), 1)

    kf = k_ref[...].astype(F32).reshape(n_blocks, KB, LANES)
    kmean = jnp.sum(kf, axis=1) * (1.0 / KB)
    km_hi, km_lo = _split_bf16(kmean)

    qh, bias = [], []
    for h in range(2):
        head_rows = (drow < MOBA_HEAD_DIM) if h == 0 else (drow >= MOBA_HEAD_DIM)
        qh_h = jnp.where(head_rows, qt, jnp.zeros_like(qt))
        bs = _dot(km_hi, qh_h) + _dot(km_lo, qh_h)
        cnt = jnp.zeros((n_blocks, S), jnp.int32)
        for jp in range(n_blocks):
            cj = bs[jp:jp + 1, :]
            beats = (cj > bs) | ((cj == bs) & (jp < blk))
            cnt = cnt + jnp.where(beats & (jp < cur), 1, 0)
        qh.append(qh_h)
        bias.append(jnp.where((blk < cur) & (cnt < MOBA_TOPK), 0.0, NEG_INF))

    steps = [(i, h) for i in range(n_blocks) for h in range(2)]
    n_slots = s_scr.shape[0]
    state = [dict(m=None, acc=None) for _ in steps]

    def block_bias(n, j):
        i, h = steps[n]
        return bias[h][j:j + 1, i * KB:(i + 1) * KB]

    def scores(n):
        i, h = steps[n]
        qi = qh[h][:, i * KB:(i + 1) * KB]
        state[n]["s"] = _dot(k_ref[0:(i + 1) * KB, :], qi)

    def pass1_block(n, j):
        i, _ = steps[n]
        st = state[n]
        s = st["s"][j * KB:(j + 1) * KB, :]
        if j == i:
            s = jnp.where(key_i <= qry_i, s, NEG_INF)
        s_scr[n % n_slots, j * KB:(j + 1) * KB, :] = s
        cm = jnp.max(s, axis=0, keepdims=True)
        if j < i:
            cm = cm + block_bias(n, j)
        st["m"] = cm if st["m"] is None else jnp.maximum(st["m"], cm)

    def pass2_block(n, j):
        i, _ = steps[n]
        m = state[n]["m"]
        shift = -m if j == i else block_bias(n, j) - m
        p = jnp.exp2(s_scr[n % n_slots, j * KB:(j + 1) * KB, :] + shift)
        p_scr[n % n_slots, j * KB:(j + 1) * KB, :] = p.astype(BF16)

    def finish(n):
        i, h = steps[n]
        state[n]["acc"] = _dot(vt_scr[h, :, 0:(i + 1) * KB],
                               p_scr[n % n_slots, 0:(i + 1) * KB, :])
        state[n]["s"] = None
        if h == 1:
            a0, a1 = state[n - 1]["acc"], state[n]["acc"]
            ot = jnp.concatenate([a0[:hd] / a0[hd:hd + 1], a1[:hd] / a1[hd:hd + 1]], axis=0)
            o_ref[i * KB:(i + 1) * KB, :] = ot.T.astype(BF16)

    scores(0)
    for j in range(steps[0][0] + 1):
        pass1_block(0, j)
    for n in range(len(steps)):
        cur_blocks = steps[n][0] + 1
        nxt_blocks = steps[n + 1][0] + 1 if n + 1 < len(steps) else 0
        if nxt_blocks:
            scores(n + 1)
        for j in range(max(cur_blocks, nxt_blocks)):
            if j < nxt_blocks:
                pass1_block(n + 1, j)
            if j < cur_blocks:
                pass2_block(n, j)
        finish(n)


def _moba_attention(k, qvt, batch, seq):
    T = batch * seq
    nb = seq // MOBA_BLOCK
    n_pairs = MOBA_HEADS // 2
    return pl.pallas_call(
        functools.partial(_moba_kernel, n_blocks=nb),
        grid=(batch, n_pairs),
        in_specs=[pl.BlockSpec((LANES, seq), lambda b, hp: (hp, b)),
                  pl.BlockSpec((seq, LANES), lambda b, hp: (b, hp)),
                  pl.BlockSpec((LANES, seq), lambda b, hp: (n_pairs + hp, b))],
        out_specs=pl.BlockSpec((seq, LANES), lambda b, hp: (b, hp)),
        out_shape=jax.ShapeDtypeStruct((T, D_MODEL), BF16),
        scratch_shapes=[pltpu.VMEM((4, seq, MOBA_BLOCK), F32),
                        pltpu.VMEM((4, seq, MOBA_BLOCK), BF16),
                        pltpu.VMEM((2, MOBA_HEAD_DIM + BF16_SUBLANES, seq), BF16)],
        compiler_params=_params(("parallel", "parallel")),
        name="moba_attn",
    )(qvt, k, qvt)


def _proj_ln_kernel(a_ref, w_ref, x_ref, g_ref, b_ref, o_ref):
    y = DEEPNORM_ALPHA * x_ref[...] + _dot(a_ref[...], w_ref[...])
    o_ref[...] = _layer_norm(y, g_ref[...], b_ref[...])


def _proj_ln(a, w_bf16, x, g, b):
    T = x.shape[0]
    row = lambda i: (i, 0)
    fixed = lambda i: (0, 0)
    return pl.pallas_call(
        _proj_ln_kernel,
        grid=(T // ROW_TILE,),
        in_specs=[pl.BlockSpec((ROW_TILE, D_MODEL), row),
                  pl.BlockSpec((D_MODEL, D_MODEL), fixed),
                  pl.BlockSpec((ROW_TILE, D_MODEL), row),
                  pl.BlockSpec((1, D_MODEL), fixed),
                  pl.BlockSpec((1, D_MODEL), fixed)],
        out_specs=pl.BlockSpec((ROW_TILE, D_MODEL), row),
        out_shape=jax.ShapeDtypeStruct((T, D_MODEL), F32),
        compiler_params=_params(("parallel",)),
        name="moba_out_ln",
    )(a, w_bf16, x, g.reshape(1, -1), b.reshape(1, -1))


def _gelu(z):
    return 0.5 * z * (1.0 + lax.erf(z * (2.0 ** -0.5)))


def _gmlp_kernel(x_ref, win_ref, bin_ref, lg_ref, lb_ref, ws_ref, bs_ref, wout_ref,
                 g1_ref, b1_ref, o_ref, u_scr, v_scr, gate_scr):
    tm = x_ref.shape[0]
    cw = GMLP_DV // 4
    xb = x_ref[...].astype(BF16)
    for c in range(8):
        z = _dot(xb, win_ref[:, c * cw:(c + 1) * cw]) + bin_ref[:, c * cw:(c + 1) * cw]
        z = _gelu(z)
        if c < 4:
            u_scr[:, c * cw:(c + 1) * cw] = z.astype(BF16)
        else:
            v_scr[:, (c - 4) * cw:(c - 3) * cw] = z
    vn = _layer_norm(v_scr[...], lg_ref[...], lb_ref[...]).astype(BF16)

    row = lax.broadcasted_iota(jnp.int32, (GMLP_CHUNK, GMLP_CHUNK), 0)
    col = lax.broadcasted_iota(jnp.int32, (GMLP_CHUNK, GMLP_CHUNK), 1)
    gd = GMLP_GROUP_DIM
    for g in range(GMLP_GROUPS):
        wc = jnp.where(col <= row, ws_ref[g], 0.0).astype(BF16)
        bias = bs_ref[:, g:g + 1]
        for c in range(tm // GMLP_CHUNK):
            r0 = c * GMLP_CHUNK
            mixed = _dot(wc, vn[r0:r0 + GMLP_CHUNK, g * gd:(g + 1) * gd]) + bias
            u = u_scr[r0:r0 + GMLP_CHUNK, g * gd:(g + 1) * gd].astype(F32)
            gate_scr[r0:r0 + GMLP_CHUNK, g * gd:(g + 1) * gd] = (u * mixed).astype(BF16)
    y = DEEPNORM_ALPHA * x_ref[...] + _dot(gate_scr[...], wout_ref[...])
    o_ref[...] = _layer_norm(y, g1_ref[...], b1_ref[...])


def _gmlp_layer(x, w_in, b_in, ln_g, ln_b, w_s, b_s_t, w_out, g1, b1):
    T = x.shape[0]
    row = lambda i: (i, 0)
    fixed = lambda i: (0, 0)
    once = pl.Buffered(1)
    return pl.pallas_call(
        _gmlp_kernel,
        grid=(T // ROW_TILE,),
        in_specs=[pl.BlockSpec((ROW_TILE, D_MODEL), row),
                  pl.BlockSpec((D_MODEL, 2 * GMLP_DV), fixed, pipeline_mode=once),
                  pl.BlockSpec((1, 2 * GMLP_DV), fixed),
                  pl.BlockSpec((1, GMLP_DV), fixed),
                  pl.BlockSpec((1, GMLP_DV), fixed),
                  pl.BlockSpec((GMLP_GROUPS, GMLP_CHUNK, GMLP_CHUNK), lambda i: (0, 0, 0)),
                  pl.BlockSpec((GMLP_CHUNK, GMLP_GROUPS), fixed),
                  pl.BlockSpec((GMLP_DV, D_MODEL), fixed, pipeline_mode=once),
                  pl.BlockSpec((1, D_MODEL), fixed),
                  pl.BlockSpec((1, D_MODEL), fixed)],
        out_specs=pl.BlockSpec((ROW_TILE, D_MODEL), row),
        out_shape=jax.ShapeDtypeStruct((T, D_MODEL), F32),
        scratch_shapes=[pltpu.VMEM((ROW_TILE, GMLP_DV), BF16),
                        pltpu.VMEM((ROW_TILE, GMLP_DV), F32),
                        pltpu.VMEM((ROW_TILE, GMLP_DV), BF16)],
        compiler_params=_params(("parallel",)),
        name="gmlp",
    )(x, w_in, b_in.reshape(1, -1), ln_g.reshape(1, -1), ln_b.reshape(1, -1), w_s, b_s_t,
      w_out, g1.reshape(1, -1), b1.reshape(1, -1))


def _router_kernel(x_ref, w_ref, b_ref, ids_ref, meta_ref, cnt_ref, carry):
    tm = x_ref.shape[0]

    @pl.when(pl.program_id(0) == 0)
    def _():
        carry[...] = jnp.zeros_like(carry)

    xh, xl = _split_bf16(x_ref[...])
    wh, wl = _split_bf16(w_ref[...])
    hi = _dot(xh, jnp.concatenate([wh, wl], axis=1))
    logits = (hi[:, :LANES] + hi[:, LANES:]) + _dot(xl, wh) + b_ref[...]
    lane_i = lax.broadcasted_iota(jnp.int32, (tm, LANES), 1)
    lane = lane_i.astype(F32)

    def first_max(vals):
        top = jnp.max(vals, axis=-1, keepdims=True)
        idx = jnp.min(jnp.where(vals == top, lane, float(LANES)), axis=-1, keepdims=True)
        return top, idx

    gl = jnp.where(lane_i < ROUTER_EXPERT_LANE0, logits, NEG_INF)
    g_top, g_idx = first_max(gl)
    g_p = 1.0 / jnp.sum(jnp.exp(gl - g_top), axis=-1, keepdims=True)
    lo = ROUTER_EXPERT_LANE0 + g_idx * MOE_EXPERTS_PER_GROUP
    el = jnp.where((lane >= lo) & (lane < lo + MOE_EXPERTS_PER_GROUP), logits, NEG_INF)
    e1, i1 = first_max(el)
    e2, i2 = first_max(jnp.where(lane == i1, NEG_INF, el))
    d = jnp.exp(e2 - e1)
    gate1 = g_p / (1.0 + d)
    gate2 = g_p * d / (1.0 + d)

    r = lax.broadcasted_iota(jnp.int32, (tm, tm), 0)
    c = lax.broadcasted_iota(jnp.int32, (tm, tm), 1)
    before = jnp.where(c < r, 1.0, 0.0).astype(BF16)
    oh = lane == g_idx
    ohf = jnp.where(oh, 1.0, 0.0)
    ahead = _dot(before, ohf.astype(BF16))
    base = carry[...]
    rank = jnp.sum(jnp.where(oh, base + ahead, 0.0), axis=-1, keepdims=True)
    total = base + jnp.sum(ohf, axis=0, keepdims=True)
    carry[...] = total
    cnt_ref[...] = jnp.broadcast_to(total, cnt_ref.shape)

    id_lane = lax.broadcasted_iota(jnp.int32, ids_ref.shape, 1)
    ids_ref[...] = jnp.where(id_lane == 0, g_idx.astype(jnp.int32),
                   jnp.where(id_lane == 1, rank.astype(jnp.int32), 0))
    meta_ref[...] = jnp.where(lane_i == META_E1, i1 - lo,
                    jnp.where(lane_i == META_E2, i2 - lo,
                    jnp.where(lane_i == META_G1, gate1,
                    jnp.where(lane_i == META_G2, gate2, 0.0))))


def _router(x, w_pad, b_pad):
    T = x.shape[0]
    row = lambda i: (i, 0)
    fixed = lambda i: (0, 0)
    return pl.pallas_call(
        _router_kernel,
        grid=(T // ROW_TILE,),
        in_specs=[pl.BlockSpec((ROW_TILE, D_MODEL), row),
                  pl.BlockSpec((D_MODEL, LANES), fixed),
                  pl.BlockSpec((1, LANES), fixed)],
        out_specs=[pl.BlockSpec((ROW_TILE, ROUTER_ID_LANES), row),
                   pl.BlockSpec((ROW_TILE, LANES), row),
                   pl.BlockSpec((8, LANES), fixed)],
        out_shape=[jax.ShapeDtypeStruct((T, ROUTER_ID_LANES), jnp.int32),
                   jax.ShapeDtypeStruct((T, LANES), F32),
                   jax.ShapeDtypeStruct((8, LANES), F32)],
        scratch_shapes=[pltpu.VMEM((1, LANES), F32)],
        compiler_params=_params(("arbitrary",)),
        name="moe_router",
    )(x, w_pad, b_pad)


def _tile_row(ref, row, sublanes):
    return ref.at[pl.ds(pl.multiple_of(row * sublanes, sublanes), sublanes)]


def _tile_row_copy(src_ref, src_row, dst_ref, dst_row, sem, sublanes):
    return pltpu.make_async_copy(_tile_row(src_ref, src_row, sublanes),
                                 _tile_row(dst_ref, dst_row, sublanes), sem)


def _tile_rows_wait(src_ref, dst_ref, n_rows, sem, sublanes):
    n = n_rows * sublanes
    pltpu.make_async_copy(src_ref.at[pl.ds(0, n)], dst_ref.at[pl.ds(0, n)], sem).wait()


def _load_tile_rows(ref, n_rows, sublanes):
    return jnp.concatenate([ref[pl.ds(c, n_rows, stride=sublanes), :]
                            for c in range(ROW_SUBLANES)], axis=1)


def _store_tile_rows(ref, val, sublanes):
    for c in range(ROW_SUBLANES):
        ref[pl.ds(c, val.shape[0], stride=sublanes), :] = val[:, c * LANES:(c + 1) * LANES]


def _dispatch_kernel(dest_ref, zero_ref, x_ref, meta_ref, buf_hbm, row_scr, sem):
    tm = x_ref.shape[0]
    ts = TOKEN_SUBLANES
    i = pl.program_id(0)
    n = pl.num_programs(0)
    slot = i % 2

    @pl.when(i == 0)
    def _():
        row_scr[...] = jnp.zeros_like(row_scr)

        def zero_block(b, start):
            @pl.when(zero_ref[b] != 0)
            def _():
                for k in range(MOE_ROWS // tm):
                    rows = pl.ds((b * MOE_ROWS + k * tm) * ts, tm * ts)
                    cp = pltpu.make_async_copy(row_scr.at[1], buf_hbm.at[rows], sem.at[1])
                    if start:
                        cp.start()
                    else:
                        cp.wait()
        for b in range(zero_ref.shape[0]):
            zero_block(b, True)
        for b in range(zero_ref.shape[0]):
            zero_block(b, False)

    stage = row_scr.at[slot]

    @pl.when(i >= 2)
    def _():
        _tile_rows_wait(stage, buf_hbm, tm, sem.at[slot], ts)

    _store_tile_rows(stage, x_ref[...], ts)
    stage[pl.ds(ROW_SUBLANES, tm, stride=ts), :] = meta_ref[...]

    def issue(r, _):
        _tile_row_copy(stage, r, buf_hbm, dest_ref[i * tm + r], sem.at[slot], ts).start()
        return 0
    lax.fori_loop(0, tm, issue, 0, unroll=ISSUE_UNROLL)

    @pl.when(i == n - 1)
    def _():
        _tile_rows_wait(stage, buf_hbm, tm, sem.at[slot], ts)

        @pl.when(n >= 2)
        def _():
            _tile_rows_wait(row_scr.at[1 - slot], buf_hbm, tm, sem.at[1 - slot], ts)


def _dispatch(dest, zero_flag, x, meta, n_rows):
    T = x.shape[0]
    return pl.pallas_call(
        _dispatch_kernel,
        grid_spec=pltpu.PrefetchScalarGridSpec(
            num_scalar_prefetch=2,
            grid=(T // COMBINE_TILE,),
            in_specs=[pl.BlockSpec((COMBINE_TILE, D_MODEL), lambda i, d, z: (i, 0)),
                      pl.BlockSpec((COMBINE_TILE, LANES), lambda i, d, z: (i, 0))],
            out_specs=pl.BlockSpec(memory_space=pl.ANY),
            scratch_shapes=[pltpu.VMEM((2, COMBINE_TILE * TOKEN_SUBLANES, LANES), F32),
                            pltpu.SemaphoreType.DMA((2,))]),
        out_shape=jax.ShapeDtypeStruct((n_rows * TOKEN_SUBLANES, LANES), F32),
        compiler_params=pltpu.CompilerParams(dimension_semantics=("arbitrary",),
                                             has_side_effects=True),
        name="moe_dispatch",
    )(dest, zero_flag, x, meta)

def _ffn_kernel(bg_ref, nused_ref, nxt_ref, rows_ref, w1_hbm, w3_hbm, w2_hbm, o_ref,
                w1_f32, w3_f32, w2_f32, w1_ref, w3_ref, w2_ref, sem, *, layer):
    i = pl.program_id(0)
    used = i < nused_ref[0]
    epg = MOE_EXPERTS_PER_GROUP

    def fetch(group):
        first = pl.multiple_of(group * epg, epg)
        return [pltpu.make_async_copy(src.at[layer, pl.ds(first, epg)], dst, sem.at[k])
                for k, (src, dst) in enumerate(((w1_hbm, w1_f32), (w3_hbm, w3_f32),
                                                (w2_hbm, w2_f32)))]

    @pl.when(i == 0)
    def _():
        for cp in fetch(bg_ref[0]):
            cp.start()

    @pl.when(used & ((i == 0) | (bg_ref[i] != bg_ref[jnp.maximum(i - 1, 0)])))
    def _():
        for cp in fetch(bg_ref[i]):
            cp.wait()
        for e in range(epg):
            w1_ref[e] = w1_f32[e].astype(BF16)
            w3_ref[e] = w3_f32[e].astype(BF16)
            w2_ref[e] = w2_f32[e].astype(BF16)

        @pl.when(nxt_ref[i] >= 0)
        def _():
            for cp in fetch(nxt_ref[i]):
                cp.start()

    @pl.when(used)
    def _():
        x = _load_tile_rows(rows_ref, MOE_ROWS, TOKEN_SUBLANES).astype(BF16)
        meta = rows_ref[pl.ds(ROW_SUBLANES, MOE_ROWS, stride=TOKEN_SUBLANES), :]
        e1 = meta[:, META_E1:META_E1 + 1]
        e2 = meta[:, META_E2:META_E2 + 1]
        g1 = meta[:, META_G1:META_G1 + 1]
        g2 = meta[:, META_G2:META_G2 + 1]
        acc = None
        for e in range(MOE_EXPERTS_PER_GROUP):
            h1 = _dot(x, w1_ref[e])
            h3 = _dot(x, w3_ref[e])
            gate = jnp.where(e1 == e, g1, g2)
            h = h1 * (1.0 / (1.0 + jnp.exp(-h1))) * h3 * gate
            h = jnp.where((e1 == e) | (e2 == e), h, 0.0).astype(BF16)
            y = _dot(h, w2_ref[e])
            acc = y if acc is None else acc + y
        _store_tile_rows(o_ref, acc, ROW_SUBLANES)

    @pl.when(jnp.logical_not(used))
    def _():
        o_ref[...] = jnp.zeros_like(o_ref)


def _expert_ffn(blk_group, n_used, next_group, buf, w1, w3, w2, layer):
    n_rows = buf.shape[0] // TOKEN_SUBLANES
    epg = MOE_EXPERTS_PER_GROUP
    up = (epg, D_MODEL, MOE_D_EXPERT)
    down = (epg, MOE_D_EXPERT, D_MODEL)
    return pl.pallas_call(
        functools.partial(_ffn_kernel, layer=layer),
        grid_spec=pltpu.PrefetchScalarGridSpec(
            num_scalar_prefetch=3,
            grid=(n_rows // MOE_ROWS,),
            in_specs=[pl.BlockSpec((MOE_ROWS * TOKEN_SUBLANES, LANES),
                                   lambda i, bg, nu, nx: (jnp.minimum(i, nu[0] - 1), 0)),
                      pl.BlockSpec(memory_space=pl.ANY),
                      pl.BlockSpec(memory_space=pl.ANY),
                      pl.BlockSpec(memory_space=pl.ANY)],
            out_specs=pl.BlockSpec((MOE_ROWS * ROW_SUBLANES, LANES),
                                   lambda i, bg, nu, nx: (i, 0)),
            scratch_shapes=[pltpu.VMEM(up, F32), pltpu.VMEM(up, F32), pltpu.VMEM(down, F32),
                            pltpu.VMEM(up, BF16), pltpu.VMEM(up, BF16), pltpu.VMEM(down, BF16),
                            pltpu.SemaphoreType.DMA((3,))]),
        out_shape=jax.ShapeDtypeStruct((n_rows * ROW_SUBLANES, LANES), F32),
        compiler_params=_params(("arbitrary",), vmem=FFN_VMEM_LIMIT),
        name="moe_ffn",
    )(blk_group, n_used, next_group, buf, w1, w3, w2)


def _combine_kernel(dest_ref, y_hbm, x_ref, g_ref, b_ref, o_ref, y_scr, sem):
    tm = x_ref.shape[0]
    i = pl.program_id(0)
    last = pl.num_programs(0) - 1
    slot = i % 2
    nxt = 1 - slot

    @pl.when(i == 0)
    def _():
        def issue(r, _):
            _tile_row_copy(y_hbm, dest_ref[r], y_scr.at[0], r, sem.at[0], ROW_SUBLANES).start()
            return 0
        lax.fori_loop(0, tm, issue, 0, unroll=ISSUE_UNROLL)

    _tile_rows_wait(y_hbm, y_scr.at[slot], tm, sem.at[slot], ROW_SUBLANES)

    base = jnp.minimum(i + 1, last) * tm
    for r in range(tm):
        _tile_row_copy(y_hbm, dest_ref[base + r], y_scr.at[nxt], r, sem.at[nxt],
                       ROW_SUBLANES).start()
    y = DEEPNORM_ALPHA * x_ref[...] + _load_tile_rows(y_scr.at[slot], tm, ROW_SUBLANES)
    o_ref[...] = _layer_norm(y, g_ref[...], b_ref[...])

    @pl.when(i == last)
    def _():
        _tile_rows_wait(y_hbm, y_scr.at[nxt], tm, sem.at[nxt], ROW_SUBLANES)


def _combine_ln(dest, y_buf, x, g, b):
    T = x.shape[0]
    row = lambda i, d: (i, 0)
    fixed = lambda i, d: (0, 0)
    return pl.pallas_call(
        _combine_kernel,
        grid_spec=pltpu.PrefetchScalarGridSpec(
            num_scalar_prefetch=1,
            grid=(T // COMBINE_TILE,),
            in_specs=[pl.BlockSpec(memory_space=pl.ANY),
                      pl.BlockSpec((COMBINE_TILE, D_MODEL), row),
                      pl.BlockSpec((1, D_MODEL), fixed),
                      pl.BlockSpec((1, D_MODEL), fixed)],
            out_specs=pl.BlockSpec((COMBINE_TILE, D_MODEL), row),
            scratch_shapes=[pltpu.VMEM((2, COMBINE_TILE * ROW_SUBLANES, LANES), F32),
                            pltpu.SemaphoreType.DMA((2,))]),
        out_shape=jax.ShapeDtypeStruct((T, D_MODEL), F32),
        compiler_params=_params(("arbitrary",)),
        name="moe_combine_ln",
    )(dest, y_buf, x, g.reshape(1, -1), b.reshape(1, -1))


def _hier_moe_ln(x, w_grp, b_grp, w_rt, b_rt, w1, w3, w2, layer, g2, b2):
    T = x.shape[0]
    pad_w = LANES - MOE_GROUPS - MOE_EXPERTS
    w_pad = jnp.concatenate([w_grp, w_rt, jnp.zeros((D_MODEL, pad_w), F32)], axis=1)
    b_pad = jnp.concatenate([b_grp, b_rt, jnp.zeros((pad_w,), F32)]).reshape(1, LANES)
    ids, meta, cnt = _router(x, w_pad, b_pad)

    counts = cnt[0, ROUTER_GROUP_LANE0:ROUTER_GROUP_LANE0 + MOE_GROUPS].astype(jnp.int32)
    padded = ((counts + MOE_ROWS - 1) // MOE_ROWS) * MOE_ROWS
    pad_end = jnp.cumsum(padded)
    pad_start = pad_end - padded
    dest = pad_start[ids[:, 0]] + ids[:, 1]
    n_blk = T // MOE_ROWS + MOE_GROUPS
    n_used = (pad_end[-1:] // MOE_ROWS).astype(jnp.int32)
    blk_row0 = jnp.minimum(jnp.arange(n_blk), n_used - 1) * MOE_ROWS
    blk_group = jnp.sum(pad_end[None, :] <= blk_row0[:, None], axis=1).astype(jnp.int32)
    blk_end = (jnp.arange(n_blk) + 1) * MOE_ROWS
    group_last = jnp.any((blk_end[:, None] == pad_end[None, :]) & (padded[None, :] > 0), axis=1)
    zero_flag = (group_last | (jnp.arange(n_blk) >= n_used)).astype(jnp.int32)

    after = pad_end[blk_group] // MOE_ROWS
    next_group = jnp.where(after < n_used, blk_group[jnp.minimum(after, n_blk - 1)], -1)
    next_group = next_group.astype(jnp.int32)

    buf = _dispatch(dest, zero_flag, x, meta, n_blk * MOE_ROWS)
    y_buf = _expert_ffn(blk_group, n_used, next_group, buf, w1, w3, w2, layer)
    return _combine_ln(dest, y_buf, x, g2, b2)


def kernel(x, moba_w_qkv, moba_w_o, gmlp_w_in, gmlp_b_in, gmlp_ln_g, gmlp_ln_b, gmlp_w_s,
           gmlp_b_s, gmlp_w_out, ln1_g, ln1_b, ln2_g, ln2_b, moe_w_grp, moe_b_grp, moe_w_rt,
           moe_b_rt, moe_w1, moe_w3, moe_w2):
    B, S, D = x.shape
    assert D == D_MODEL and S % MOBA_BLOCK == 0 and (B * S) % ROW_TILE == 0
    xf = x.reshape(B * S, D)
    for i in range(DEPTH):
        j = i // 2
        if i % 2 == 0:
            wk = moba_w_qkv[j, :, D:2 * D].astype(BF16)
            wqvt = jnp.concatenate([moba_w_qkv[j, :, :D], moba_w_qkv[j, :, 2 * D:]],
                                   axis=1).T.astype(BF16)
            k, qvt = _qkv_proj(xf, wk, wqvt)
            att = _moba_attention(k, qvt, B, S)
            xf = _proj_ln(att, moba_w_o[j].astype(BF16), xf, ln1_g[i], ln1_b[i])
        else:
            xf = _gmlp_layer(xf, gmlp_w_in[j].astype(BF16), gmlp_b_in[j], gmlp_ln_g[j],
                             gmlp_ln_b[j], gmlp_w_s[j], gmlp_b_s[j].T,
                             gmlp_w_out[j].astype(BF16), ln1_g[i], ln1_b[i])
        xf = _hier_moe_ln(xf, moe_w_grp[i], moe_b_grp[i], moe_w_rt[i], moe_b_rt[i],
                          moe_w1, moe_w3, moe_w2, i, ln2_g[i], ln2_b[i])
    return xf.reshape(B, S, D)
```

```python
import functools

import jax
import jax.numpy as jnp
from jax import lax
from jax.experimental import pallas as pl
from jax.experimental.pallas import tpu as pltpu

F32 = jnp.float32
BF16 = jnp.bfloat16

D_MODEL = 1024
DEPTH = 4
MOBA_HEADS = 16
MOBA_HEAD_DIM = 64
MOBA_BLOCK = 256
MOBA_TOPK = 3
GMLP_DV = 3072
GMLP_GROUPS = 8
GMLP_GROUP_DIM = GMLP_DV // GMLP_GROUPS
GMLP_CHUNK = 128
MOE_GROUPS = 8
MOE_EXPERTS_PER_GROUP = 8
MOE_EXPERTS = 64
MOE_D_EXPERT = 256
LN_EPS = 1e-5
DEEPNORM_ALPHA = (2.0 * DEPTH) ** 0.25

LANES = 128
BF16_SUBLANES = 16
ROW_SUBLANES = D_MODEL // LANES
TOKEN_SUBLANES = 2 * ROW_SUBLANES
LOG2_E = 1.4426950408889634
ROUTER_GROUP_LANE0 = 0
ROUTER_EXPERT_LANE0 = 8
ROUTER_ID_ROWS = 8
MOE_ROWS = 512
META_E1, META_E2, META_G1, META_G2 = 0, 1, 2, 3
ROW_TILE = 512
COMBINE_TILE = 512
ISSUE_UNROLL = 8
VMEM_LIMIT = 56 * 1024 * 1024
FFN_VMEM_LIMIT = 62 * 1024 * 1024
NEG_INF = float("-inf")


def _params(sem, vmem=VMEM_LIMIT):
    return pltpu.CompilerParams(dimension_semantics=sem, vmem_limit_bytes=vmem)


def _layer_norm(y, g, b):
    mu = jnp.mean(y, axis=-1, keepdims=True)
    yc = y - mu
    var = jnp.mean(yc * yc, axis=-1, keepdims=True)
    return yc * lax.rsqrt(var + LN_EPS) * g + b


def _dot(a, b):
    return jnp.dot(a, b, preferred_element_type=F32)


def _dot_nt(a, b):
    return lax.dot_general(a, b, (((1,), (1,)), ((), ())), preferred_element_type=F32)


def _split_bf16(x):
    hi = x.astype(BF16)
    lo = (x - hi.astype(F32)).astype(BF16)
    return hi, lo


def _qkv_kernel(x_ref, wk_ref, wqvt_ref, k_ref, qvt_ref):
    xb = x_ref[...].astype(BF16)
    k_ref[...] = _dot(xb, wk_ref[...]).astype(BF16)
    qt = _dot_nt(wqvt_ref[:D_MODEL, :], xb)
    qvt_ref[:D_MODEL, :] = (qt * (MOBA_HEAD_DIM ** -0.5 * LOG2_E)).astype(BF16)
    qvt_ref[D_MODEL:, :] = _dot_nt(wqvt_ref[D_MODEL:, :], xb).astype(BF16)


def _qkv_proj(x, wk_bf16, wqvt_bf16):
    T = x.shape[0]
    return pl.pallas_call(
        _qkv_kernel,
        grid=(T // ROW_TILE,),
        in_specs=[pl.BlockSpec((ROW_TILE, D_MODEL), lambda i: (i, 0)),
                  pl.BlockSpec((D_MODEL, D_MODEL), lambda i: (0, 0)),
                  pl.BlockSpec((2 * D_MODEL, D_MODEL), lambda i: (0, 0))],
        out_specs=[pl.BlockSpec((ROW_TILE, D_MODEL), lambda i: (i, 0)),
                   pl.BlockSpec((2 * D_MODEL, ROW_TILE), lambda i: (0, i))],
        out_shape=[jax.ShapeDtypeStruct((T, D_MODEL), BF16),
                   jax.ShapeDtypeStruct((2 * D_MODEL, T), BF16)],
        compiler_params=_params(("parallel",)),
        name="moba_qkv",
    )(x, wk_bf16, wqvt_bf16)


def _moba_kernel(qt_ref, k_ref, vt_ref, o_ref, s_scr, p_scr, vt_scr, *, n_blocks):
    KB = MOBA_BLOCK
    S = n_blocks * KB
    hd = MOBA_HEAD_DIM
    for h in range(2):
        vt_scr[h, :hd, :] = vt_ref[h * hd:(h + 1) * hd, :]
        vt_scr[h, hd:, :] = jnp.ones((vt_scr.shape[1] - hd, S), BF16)
    qt = qt_ref[...]
    drow = lax.broadcasted_iota(jnp.int32, (LANES, S), 0)
    blk = lax.broadcasted_iota(jnp.int32, (n_blocks, S), 0)
    cur = jnp.right_shift(lax.broadcasted_iota(jnp.int32, (n_blocks, S), 1),
                          KB.bit_length() - 1)
    key_i = lax.broadcasted_iota(jnp.int32, (KB, KB), 0)
    qry_i = lax.broadcasted_iota(jnp.int32, (KB, KB), 1)

    kf = k_ref[...].astype(F32).reshape(n_blocks, KB, LANES)
    kmean = jnp.sum(kf, axis=1) * (1.0 / KB)
    km_hi, km_lo = _split_bf16(kmean)

    qh, bias = [], []
    for h in range(2):
        head_rows = (drow < MOBA_HEAD_DIM) if h == 0 else (drow >= MOBA_HEAD_DIM)
        qh_h = jnp.where(head_rows, qt, jnp.zeros_like(qt))
        bs = _dot(km_hi, qh_h) + _dot(km_lo, qh_h)
        cnt = jnp.zeros((n_blocks, S), jnp.int32)
        for jp in range(n_blocks):
            cj = bs[jp:jp + 1, :]
            beats = (cj > bs) | ((cj == bs) & (jp < blk))
            cnt = cnt + jnp.where(beats & (jp < cur), 1, 0)
        qh.append(qh_h)
        bias.append(jnp.where((blk < cur) & (cnt < MOBA_TOPK), 0.0, NEG_INF))

    steps = [(i, h) for i in range(n_blocks) for h in range(2)]
    n_slots = s_scr.shape[0]
    state = [dict(m=None, acc=None) for _ in steps]

    def block_bias(n, j):
        i, h = steps[n]
        return bias[h][j:j + 1, i * KB:(i + 1) * KB]

    def scores(n):
        i, h = steps[n]
        qi = qh[h][:, i * KB:(i + 1) * KB]
        state[n]["s"] = _dot(k_ref[0:(i + 1) * KB, :], qi)

    def pass1_block(n, j):
        i, _ = steps[n]
        st = state[n]
        s = st["s"][j * KB:(j + 1) * KB, :]
        if j == i:
            s = jnp.where(key_i <= qry_i, s, NEG_INF)
        s_scr[n % n_slots, j * KB:(j + 1) * KB, :] = s
        cm = jnp.max(s, axis=0, keepdims=True)
        if j < i:
            cm = cm + block_bias(n, j)
        st["m"] = cm if st["m"] is None else jnp.maximum(st["m"], cm)

    def pass2_block(n, j):
        i, _ = steps[n]
        m = state[n]["m"]
        shift = -m if j == i else block_bias(n, j) - m
        p = jnp.exp2(s_scr[n % n_slots, j * KB:(j + 1) * KB, :] + shift)
        p_scr[n % n_slots, j * KB:(j + 1) * KB, :] = p.astype(BF16)

    def finish(n):
        i, h = steps[n]
        state[n]["acc"] = _dot(vt_scr[h, :, 0:(i + 1) * KB],
                               p_scr[n % n_slots, 0:(i + 1) * KB, :])
        state[n]["s"] = None
        if h == 1:
            a0, a1 = state[n - 1]["acc"], state[n]["acc"]
            ot = jnp.concatenate([a0[:hd] / a0[hd:hd + 1], a1[:hd] / a1[hd:hd + 1]], axis=0)
            o_ref[i * KB:(i + 1) * KB, :] = ot.T.astype(BF16)

    scores(0)
    for j in range(steps[0][0] + 1):
        pass1_block(0, j)
    for n in range(len(steps)):
        cur_blocks = steps[n][0] + 1
        nxt_blocks = steps[n + 1][0] + 1 if n + 1 < len(steps) else 0
        if nxt_blocks:
            scores(n + 1)
        for j in range(max(cur_blocks, nxt_blocks)):
            if j < nxt_blocks:
                pass1_block(n + 1, j)
            if j < cur_blocks:
                pass2_block(n, j)
        finish(n)


def _moba_attention(k, qvt, batch, seq):
    T = batch * seq
    nb = seq // MOBA_BLOCK
    n_pairs = MOBA_HEADS // 2
    return pl.pallas_call(
        functools.partial(_moba_kernel, n_blocks=nb),
        grid=(batch, n_pairs),
        in_specs=[pl.BlockSpec((LANES, seq), lambda b, hp: (hp, b)),
                  pl.BlockSpec((seq, LANES), lambda b, hp: (b, hp)),
                  pl.BlockSpec((LANES, seq), lambda b, hp: (n_pairs + hp, b))],
        out_specs=pl.BlockSpec((seq, LANES), lambda b, hp: (b, hp)),
        out_shape=jax.ShapeDtypeStruct((T, D_MODEL), BF16),
        scratch_shapes=[pltpu.VMEM((4, seq, MOBA_BLOCK), F32),
                        pltpu.VMEM((4, seq, MOBA_BLOCK), BF16),
                        pltpu.VMEM((2, MOBA_HEAD_DIM + BF16_SUBLANES, seq), BF16)],
        compiler_params=_params(("parallel", "parallel")),
        name="moba_attn",
    )(qvt, k, qvt)


def _proj_ln_kernel(a_ref, w_ref, x_ref, g_ref, b_ref, o_ref):
    y = DEEPNORM_ALPHA * x_ref[...] + _dot(a_ref[...], w_ref[...])
    o_ref[...] = _layer_norm(y, g_ref[...], b_ref[...])


def _proj_ln(a, w_bf16, x, g, b):
    T = x.shape[0]
    row = lambda i: (i, 0)
    fixed = lambda i: (0, 0)
    return pl.pallas_call(
        _proj_ln_kernel,
        grid=(T // ROW_TILE,),
        in_specs=[pl.BlockSpec((ROW_TILE, D_MODEL), row),
                  pl.BlockSpec((D_MODEL, D_MODEL), fixed),
                  pl.BlockSpec((ROW_TILE, D_MODEL), row),
                  pl.BlockSpec((1, D_MODEL), fixed),
                  pl.BlockSpec((1, D_MODEL), fixed)],
        out_specs=pl.BlockSpec((ROW_TILE, D_MODEL), row),
        out_shape=jax.ShapeDtypeStruct((T, D_MODEL), F32),
        compiler_params=_params(("parallel",)),
        name="moba_out_ln",
    )(a, w_bf16, x, g.reshape(1, -1), b.reshape(1, -1))


def _gelu(z):
    return 0.5 * z * (1.0 + lax.erf(z * (2.0 ** -0.5)))


def _gmlp_kernel(x_ref, win_ref, bin_ref, lg_ref, lb_ref, ws_ref, bs_ref, wout_ref,
                 g1_ref, b1_ref, o_ref, u_scr, v_scr, gate_scr):
    tm = x_ref.shape[0]
    cw = GMLP_DV // 4
    xb = x_ref[...].astype(BF16)
    for c in range(8):
        z = _dot(xb, win_ref[:, c * cw:(c + 1) * cw]) + bin_ref[:, c * cw:(c + 1) * cw]
        z = _gelu(z)
        if c < 4:
            u_scr[:, c * cw:(c + 1) * cw] = z.astype(BF16)
        else:
            v_scr[:, (c - 4) * cw:(c - 3) * cw] = z
    vn = _layer_norm(v_scr[...], lg_ref[...], lb_ref[...]).astype(BF16)

    row = lax.broadcasted_iota(jnp.int32, (GMLP_CHUNK, GMLP_CHUNK), 0)
    col = lax.broadcasted_iota(jnp.int32, (GMLP_CHUNK, GMLP_CHUNK), 1)
    gd = GMLP_GROUP_DIM
    for g in range(GMLP_GROUPS):
        wc = jnp.where(col <= row, ws_ref[g], 0.0).astype(BF16)
        bias = bs_ref[:, g:g + 1]
        for c in range(tm // GMLP_CHUNK):
            r0 = c * GMLP_CHUNK
            mixed = _dot(wc, vn[r0:r0 + GMLP_CHUNK, g * gd:(g + 1) * gd]) + bias
            u = u_scr[r0:r0 + GMLP_CHUNK, g * gd:(g + 1) * gd].astype(F32)
            gate_scr[r0:r0 + GMLP_CHUNK, g * gd:(g + 1) * gd] = (u * mixed).astype(BF16)
    y = DEEPNORM_ALPHA * x_ref[...] + _dot(gate_scr[...], wout_ref[...])
    o_ref[...] = _layer_norm(y, g1_ref[...], b1_ref[...])


def _gmlp_layer(x, w_in, b_in, ln_g, ln_b, w_s, b_s_t, w_out, g1, b1):
    T = x.shape[0]
    row = lambda i: (i, 0)
    fixed = lambda i: (0, 0)
    once = pl.Buffered(1)
    return pl.pallas_call(
        _gmlp_kernel,
        grid=(T // ROW_TILE,),
        in_specs=[pl.BlockSpec((ROW_TILE, D_MODEL), row),
                  pl.BlockSpec((D_MODEL, 2 * GMLP_DV), fixed, pipeline_mode=once),
                  pl.BlockSpec((1, 2 * GMLP_DV), fixed),
                  pl.BlockSpec((1, GMLP_DV), fixed),
                  pl.BlockSpec((1, GMLP_DV), fixed),
                  pl.BlockSpec((GMLP_GROUPS, GMLP_CHUNK, GMLP_CHUNK), lambda i: (0, 0, 0)),
                  pl.BlockSpec((GMLP_CHUNK, GMLP_GROUPS), fixed),
                  pl.BlockSpec((GMLP_DV, D_MODEL), fixed, pipeline_mode=once),
                  pl.BlockSpec((1, D_MODEL), fixed),
                  pl.BlockSpec((1, D_MODEL), fixed)],
        out_specs=pl.BlockSpec((ROW_TILE, D_MODEL), row),
        out_shape=jax.ShapeDtypeStruct((T, D_MODEL), F32),
        scratch_shapes=[pltpu.VMEM((ROW_TILE, GMLP_DV), BF16),
                        pltpu.VMEM((ROW_TILE, GMLP_DV), F32),
                        pltpu.VMEM((ROW_TILE, GMLP_DV), BF16)],
        compiler_params=_params(("parallel",)),
        name="gmlp",
    )(x, w_in, b_in.reshape(1, -1), ln_g.reshape(1, -1), ln_b.reshape(1, -1), w_s, b_s_t,
      w_out, g1.reshape(1, -1), b1.reshape(1, -1))


def _router_kernel(x_ref, w_ref, b_ref, ids_ref, meta_ref, cnt_ref, carry):
    tm = x_ref.shape[0]

    @pl.when(pl.program_id(0) == 0)
    def _():
        carry[...] = jnp.zeros_like(carry)

    xh, xl = _split_bf16(x_ref[...])
    wh, wl = _split_bf16(w_ref[...])
    hi = _dot(xh, jnp.concatenate([wh, wl], axis=1))
    logits = (hi[:, :LANES] + hi[:, LANES:]) + _dot(xl, wh) + b_ref[...]
    lane_i = lax.broadcasted_iota(jnp.int32, (tm, LANES), 1)
    lane = lane_i.astype(F32)

    def first_max(vals):
        top = jnp.max(vals, axis=-1, keepdims=True)
        idx = jnp.min(jnp.where(vals == top, lane, float(LANES)), axis=-1, keepdims=True)
        return top, idx

    gl = jnp.where(lane_i < ROUTER_EXPERT_LANE0, logits, NEG_INF)
    g_top, g_idx = first_max(gl)
    g_p = 1.0 / jnp.sum(jnp.exp(gl - g_top), axis=-1, keepdims=True)
    lo = ROUTER_EXPERT_LANE0 + g_idx * MOE_EXPERTS_PER_GROUP
    el = jnp.where((lane >= lo) & (lane < lo + MOE_EXPERTS_PER_GROUP), logits, NEG_INF)
    e1, i1 = first_max(el)
    e2, i2 = first_max(jnp.where(lane == i1, NEG_INF, el))
    d = jnp.exp(e2 - e1)
    gate1 = g_p / (1.0 + d)
    gate2 = g_p * d / (1.0 + d)

    r = lax.broadcasted_iota(jnp.int32, (tm, tm), 0)
    c = lax.broadcasted_iota(jnp.int32, (tm, tm), 1)
    before = jnp.where(c < r, 1.0, 0.0).astype(BF16)
    oh = lane == g_idx
    ohf = jnp.where(oh, 1.0, 0.0)
    ahead = _dot(before, ohf.astype(BF16))
    base = carry[...]
    rank = jnp.sum(jnp.where(oh, base + ahead, 0.0), axis=-1, keepdims=True)
    total = base + jnp.sum(ohf, axis=0, keepdims=True)
    carry[...] = total
    cnt_ref[...] = jnp.broadcast_to(total, cnt_ref.shape)

    id_cols = jnp.where(lane_i == 0, g_idx, jnp.where(lane_i == 1, rank, 0.0))
    ids_ref[...] = id_cols.T[:ROUTER_ID_ROWS, :].astype(jnp.int32)
    meta_ref[...] = jnp.where(lane_i == META_E1, i1 - lo,
                    jnp.where(lane_i == META_E2, i2 - lo,
                    jnp.where(lane_i == META_G1, gate1,
                    jnp.where(lane_i == META_G2, gate2, 0.0))))


def _router(x, w_pad, b_pad):
    T = x.shape[0]
    row = lambda i: (i, 0)
    fixed = lambda i: (0, 0)
    return pl.pallas_call(
        _router_kernel,
        grid=(T // ROW_TILE,),
        in_specs=[pl.BlockSpec((ROW_TILE, D_MODEL), row),
                  pl.BlockSpec((D_MODEL, LANES), fixed),
                  pl.BlockSpec((1, LANES), fixed)],
        out_specs=[pl.BlockSpec((ROUTER_ID_ROWS, ROW_TILE), lambda i: (0, i)),
                   pl.BlockSpec((ROW_TILE, LANES), row),
                   pl.BlockSpec((8, LANES), fixed)],
        out_shape=[jax.ShapeDtypeStruct((ROUTER_ID_ROWS, T), jnp.int32),
                   jax.ShapeDtypeStruct((T, LANES), F32),
                   jax.ShapeDtypeStruct((8, LANES), F32)],
        scratch_shapes=[pltpu.VMEM((1, LANES), F32)],
        compiler_params=_params(("arbitrary",)),
        name="moe_router",
    )(x, w_pad, b_pad)


def _tile_row(ref, row, sublanes):
    return ref.at[pl.ds(pl.multiple_of(row * sublanes, sublanes), sublanes)]


def _tile_row_copy(src_ref, src_row, dst_ref, dst_row, sem, sublanes):
    return pltpu.make_async_copy(_tile_row(src_ref, src_row, sublanes),
                                 _tile_row(dst_ref, dst_row, sublanes), sem)


def _tile_rows_wait(src_ref, dst_ref, n_rows, sem, sublanes):
    n = n_rows * sublanes
    pltpu.make_async_copy(src_ref.at[pl.ds(0, n)], dst_ref.at[pl.ds(0, n)], sem).wait()


def _load_tile_rows(ref, n_rows, sublanes):
    return jnp.concatenate([ref[pl.ds(c, n_rows, stride=sublanes), :]
                            for c in range(ROW_SUBLANES)], axis=1)


def _store_tile_rows(ref, val, sublanes):
    for c in range(ROW_SUBLANES):
        ref[pl.ds(c, val.shape[0], stride=sublanes), :] = val[:, c * LANES:(c + 1) * LANES]


def _dispatch_kernel(dest_ref, zero_ref, x_ref, meta_ref, buf_hbm, row_scr, sem):
    tm = x_ref.shape[0]
    ts = TOKEN_SUBLANES
    i = pl.program_id(0)
    n = pl.num_programs(0)
    slot = i % 2

    @pl.when(i == 0)
    def _():
        row_scr[...] = jnp.zeros_like(row_scr)

        def zero_block(b, start):
            @pl.when(zero_ref[b] != 0)
            def _():
                for k in range(MOE_ROWS // tm):
                    rows = pl.ds((b * MOE_ROWS + k * tm) * ts, tm * ts)
                    cp = pltpu.make_async_copy(row_scr.at[1], buf_hbm.at[rows], sem.at[1])
                    if start:
                        cp.start()
                    else:
                        cp.wait()
        for b in range(zero_ref.shape[0]):
            zero_block(b, True)
        for b in range(zero_ref.shape[0]):
            zero_block(b, False)

    stage = row_scr.at[slot]

    @pl.when(i >= 2)
    def _():
        _tile_rows_wait(stage, buf_hbm, tm, sem.at[slot], ts)

    _store_tile_rows(stage, x_ref[...], ts)
    stage[pl.ds(ROW_SUBLANES, tm, stride=ts), :] = meta_ref[...]

    def issue(r, _):
        _tile_row_copy(stage, r, buf_hbm, dest_ref[i * tm + r], sem.at[slot], ts).start()
        return 0
    lax.fori_loop(0, tm, issue, 0, unroll=ISSUE_UNROLL)

    @pl.when(i == n - 1)
    def _():
        _tile_rows_wait(stage, buf_hbm, tm, sem.at[slot], ts)

        @pl.when(n >= 2)
        def _():
            _tile_rows_wait(row_scr.at[1 - slot], buf_hbm, tm, sem.at[1 - slot], ts)


def _dispatch(dest, zero_flag, x, meta, n_rows):
    T = x.shape[0]
    return pl.pallas_call(
        _dispatch_kernel,
        grid_spec=pltpu.PrefetchScalarGridSpec(
            num_scalar_prefetch=2,
            grid=(T // COMBINE_TILE,),
            in_specs=[pl.BlockSpec((COMBINE_TILE, D_MODEL), lambda i, d, z: (i, 0)),
                      pl.BlockSpec((COMBINE_TILE, LANES), lambda i, d, z: (i, 0))],
            out_specs=pl.BlockSpec(memory_space=pl.ANY),
            scratch_shapes=[pltpu.VMEM((2, COMBINE_TILE * TOKEN_SUBLANES, LANES), F32),
                            pltpu.SemaphoreType.DMA((2,))]),
        out_shape=jax.ShapeDtypeStruct((n_rows * TOKEN_SUBLANES, LANES), F32),
        compiler_params=pltpu.CompilerParams(dimension_semantics=("arbitrary",),
                                             has_side_effects=True),
        name="moe_dispatch",
    )(dest, zero_flag, x, meta)

def _ffn_kernel(bg_ref, nused_ref, nxt_ref, rows_ref, w1_hbm, w3_hbm, w2_hbm, o_ref,
                w1_f32, w3_f32, w2_f32, w1_ref, w3_ref, w2_ref, sem, *, layer):
    i = pl.program_id(0)
    used = i < nused_ref[0]
    epg = MOE_EXPERTS_PER_GROUP

    def fetch(group):
        first = pl.multiple_of(group * epg, epg)
        return [pltpu.make_async_copy(src.at[layer, pl.ds(first, epg)], dst, sem.at[k])
                for k, (src, dst) in enumerate(((w1_hbm, w1_f32), (w3_hbm, w3_f32),
                                                (w2_hbm, w2_f32)))]

    @pl.when(i == 0)
    def _():
        for cp in fetch(bg_ref[0]):
            cp.start()

    @pl.when(used & ((i == 0) | (bg_ref[i] != bg_ref[jnp.maximum(i - 1, 0)])))
    def _():
        for cp in fetch(bg_ref[i]):
            cp.wait()
        for e in range(epg):
            w1_ref[e] = w1_f32[e].astype(BF16)
            w3_ref[e] = w3_f32[e].astype(BF16)
            w2_ref[e] = w2_f32[e].astype(BF16)

        @pl.when(nxt_ref[i] >= 0)
        def _():
            for cp in fetch(nxt_ref[i]):
                cp.start()

    @pl.when(used)
    def _():
        x = _load_tile_rows(rows_ref, MOE_ROWS, TOKEN_SUBLANES).astype(BF16)
        meta = rows_ref[pl.ds(ROW_SUBLANES, MOE_ROWS, stride=TOKEN_SUBLANES), :]
        e1 = meta[:, META_E1:META_E1 + 1]
        e2 = meta[:, META_E2:META_E2 + 1]
        g1 = meta[:, META_G1:META_G1 + 1]
        g2 = meta[:, META_G2:META_G2 + 1]
        acc = None
        for e in range(MOE_EXPERTS_PER_GROUP):
            h1 = _dot(x, w1_ref[e])
            h3 = _dot(x, w3_ref[e])
            gate = jnp.where(e1 == e, g1, g2)
            h = h1 * (1.0 / (1.0 + jnp.exp(-h1))) * h3 * gate
            h = jnp.where((e1 == e) | (e2 == e), h, 0.0).astype(BF16)
            y = _dot(h, w2_ref[e])
            acc = y if acc is None else acc + y
        _store_tile_rows(o_ref, acc, ROW_SUBLANES)

    @pl.when(jnp.logical_not(used))
    def _():
        o_ref[...] = jnp.zeros_like(o_ref)


def _expert_ffn(blk_group, n_used, next_group, buf, w1, w3, w2, layer):
    n_rows = buf.shape[0] // TOKEN_SUBLANES
    epg = MOE_EXPERTS_PER_GROUP
    up = (epg, D_MODEL, MOE_D_EXPERT)
    down = (epg, MOE_D_EXPERT, D_MODEL)
    return pl.pallas_call(
        functools.partial(_ffn_kernel, layer=layer),
        grid_spec=pltpu.PrefetchScalarGridSpec(
            num_scalar_prefetch=3,
            grid=(n_rows // MOE_ROWS,),
            in_specs=[pl.BlockSpec((MOE_ROWS * TOKEN_SUBLANES, LANES),
                                   lambda i, bg, nu, nx: (jnp.minimum(i, nu[0] - 1), 0)),
                      pl.BlockSpec(memory_space=pl.ANY),
                      pl.BlockSpec(memory_space=pl.ANY),
                      pl.BlockSpec(memory_space=pl.ANY)],
            out_specs=pl.BlockSpec((MOE_ROWS * ROW_SUBLANES, LANES),
                                   lambda i, bg, nu, nx: (i, 0)),
            scratch_shapes=[pltpu.VMEM(up, F32), pltpu.VMEM(up, F32), pltpu.VMEM(down, F32),
                            pltpu.VMEM(up, BF16), pltpu.VMEM(up, BF16), pltpu.VMEM(down, BF16),
                            pltpu.SemaphoreType.DMA((3,))]),
        out_shape=jax.ShapeDtypeStruct((n_rows * ROW_SUBLANES, LANES), F32),
        compiler_params=_params(("arbitrary",), vmem=FFN_VMEM_LIMIT),
        name="moe_ffn",
    )(blk_group, n_used, next_group, buf, w1, w3, w2)


def _combine_kernel(dest_ref, y_hbm, x_ref, g_ref, b_ref, o_ref, y_scr, sem):
    tm = x_ref.shape[0]
    i = pl.program_id(0)

    def gather(tile, slot):
        for r in range(tm):
            _tile_row_copy(y_hbm, dest_ref[tile * tm + r], y_scr.at[slot], r, sem.at[slot],
                           ROW_SUBLANES).start()

    @pl.when(i == 0)
    def _():
        gather(0, 0)

    @pl.when(i + 1 < pl.num_programs(0))
    def _():
        gather(i + 1, (i + 1) % 2)

    slot = i % 2
    _tile_rows_wait(y_hbm, y_scr.at[slot], tm, sem.at[slot], ROW_SUBLANES)
    y = DEEPNORM_ALPHA * x_ref[...] + _load_tile_rows(y_scr.at[slot], tm, ROW_SUBLANES)
    o_ref[...] = _layer_norm(y, g_ref[...], b_ref[...])


def _combine_ln(dest, y_buf, x, g, b):
    T = x.shape[0]
    row = lambda i, d: (i, 0)
    fixed = lambda i, d: (0, 0)
    return pl.pallas_call(
        _combine_kernel,
        grid_spec=pltpu.PrefetchScalarGridSpec(
            num_scalar_prefetch=1,
            grid=(T // COMBINE_TILE,),
            in_specs=[pl.BlockSpec(memory_space=pl.ANY),
                      pl.BlockSpec((COMBINE_TILE, D_MODEL), row),
                      pl.BlockSpec((1, D_MODEL), fixed),
                      pl.BlockSpec((1, D_MODEL), fixed)],
            out_specs=pl.BlockSpec((COMBINE_TILE, D_MODEL), row),
            scratch_shapes=[pltpu.VMEM((2, COMBINE_TILE * ROW_SUBLANES, LANES), F32),
                            pltpu.SemaphoreType.DMA((2,))]),
        out_shape=jax.ShapeDtypeStruct((T, D_MODEL), F32),
        compiler_params=_params(("arbitrary",)),
        name="moe_combine_ln",
    )(dest, y_buf, x, g.reshape(1, -1), b.reshape(1, -1))


def _hier_moe_ln(x, w_grp, b_grp, w_rt, b_rt, w1, w3, w2, layer, g2, b2):
    T = x.shape[0]
    pad_w = LANES - MOE_GROUPS - MOE_EXPERTS
    w_pad = jnp.concatenate([w_grp, w_rt, jnp.zeros((D_MODEL, pad_w), F32)], axis=1)
    b_pad = jnp.concatenate([b_grp, b_rt, jnp.zeros((pad_w,), F32)]).reshape(1, LANES)
    ids, meta, cnt = _router(x, w_pad, b_pad)

    counts = cnt[0, ROUTER_GROUP_LANE0:ROUTER_GROUP_LANE0 + MOE_GROUPS].astype(jnp.int32)
    padded = ((counts + MOE_ROWS - 1) // MOE_ROWS) * MOE_ROWS
    pad_end = jnp.cumsum(padded)
    pad_start = pad_end - padded
    dest = ids[1]
    for g in range(MOE_GROUPS):
        dest = dest + jnp.where(ids[0] == g, pad_start[g], 0)
    n_blk = T // MOE_ROWS + MOE_GROUPS
    n_used = (pad_end[-1:] // MOE_ROWS).astype(jnp.int32)
    blk_row0 = jnp.minimum(jnp.arange(n_blk), n_used - 1) * MOE_ROWS
    blk_group = jnp.sum(pad_end[None, :] <= blk_row0[:, None], axis=1).astype(jnp.int32)
    blk_end = (jnp.arange(n_blk) + 1) * MOE_ROWS
    group_last = jnp.any((blk_end[:, None] == pad_end[None, :]) & (padded[None, :] > 0), axis=1)
    zero_flag = (group_last | (jnp.arange(n_blk) >= n_used)).astype(jnp.int32)

    after = pad_end[blk_group] // MOE_ROWS
    next_group = jnp.where(after < n_used, blk_group[jnp.minimum(after, n_blk - 1)], -1)
    next_group = next_group.astype(jnp.int32)

    buf = _dispatch(dest, zero_flag, x, meta, n_blk * MOE_ROWS)
    y_buf = _expert_ffn(blk_group, n_used, next_group, buf, w1, w3, w2, layer)
    return _combine_ln(dest, y_buf, x, g2, b2)


def kernel(x, moba_w_qkv, moba_w_o, gmlp_w_in, gmlp_b_in, gmlp_ln_g, gmlp_ln_b, gmlp_w_s,
           gmlp_b_s, gmlp_w_out, ln1_g, ln1_b, ln2_g, ln2_b, moe_w_grp, moe_b_grp, moe_w_rt,
           moe_b_rt, moe_w1, moe_w3, moe_w2):
    B, S, D = x.shape
    assert D == D_MODEL and S % MOBA_BLOCK == 0 and (B * S) % ROW_TILE == 0
    xf = x.reshape(B * S, D)
    for i in range(DEPTH):
        j = i // 2
        if i % 2 == 0:
            wk = moba_w_qkv[j, :, D:2 * D].astype(BF16)
            wqvt = jnp.concatenate([moba_w_qkv[j, :, :D], moba_w_qkv[j, :, 2 * D:]],
                                   axis=1).T.astype(BF16)
            k, qvt = _qkv_proj(xf, wk, wqvt)
            att = _moba_attention(k, qvt, B, S)
            xf = _proj_ln(att, moba_w_o[j].astype(BF16), xf, ln1_g[i], ln1_b[i])
        else:
            xf = _gmlp_layer(xf, gmlp_w_in[j].astype(BF16), gmlp_b_in[j], gmlp_ln_g[j],
                             gmlp_ln_b[j], gmlp_w_s[j], gmlp_b_s[j].T,
                             gmlp_w_out[j].astype(BF16), ln1_g[i], ln1_b[i])
        xf = _hier_moe_ln(xf, moe_w_grp[i], moe_b_grp[i], moe_w_rt[i], moe_b_rt[i],
                          moe_w1, moe_w3, moe_w2, i, ln2_g[i], ln2_b[i])
    return xf.reshape(B, S, D)
```

```python
import functools

import jax
import jax.numpy as jnp
from jax import lax
from jax.experimental import pallas as pl
from jax.experimental.pallas import tpu as pltpu

F32 = jnp.float32
BF16 = jnp.bfloat16

D_MODEL = 1024
DEPTH = 4
MOBA_HEADS = 16
MOBA_HEAD_DIM = 64
MOBA_BLOCK = 256
MOBA_TOPK = 3
GMLP_DV = 3072
GMLP_GROUPS = 8
GMLP_GROUP_DIM = GMLP_DV // GMLP_GROUPS
GMLP_CHUNK = 128
MOE_GROUPS = 8
MOE_EXPERTS_PER_GROUP = 8
MOE_EXPERTS = 64
MOE_D_EXPERT = 256
LN_EPS = 1e-5
DEEPNORM_ALPHA = (2.0 * DEPTH) ** 0.25

LANES = 128
BF16_SUBLANES = 16
ROW_SUBLANES = D_MODEL // LANES
TOKEN_SUBLANES = 2 * ROW_SUBLANES
LOG2_E = 1.4426950408889634
ROUTER_GROUP_LANE0 = 0
ROUTER_EXPERT_LANE0 = 8
ROUTER_ID_ROWS = 8
MOE_ROWS = 512
META_E1, META_E2, META_G1, META_G2 = 0, 1, 2, 3
ROW_TILE = 512
COMBINE_TILE = 512
ISSUE_UNROLL = 8
VMEM_LIMIT = 56 * 1024 * 1024
FFN_VMEM_LIMIT = 62 * 1024 * 1024
NEG_INF = float("-inf")


def _params(sem, vmem=VMEM_LIMIT):
    return pltpu.CompilerParams(dimension_semantics=sem, vmem_limit_bytes=vmem)


def _layer_norm(y, g, b):
    mu = jnp.mean(y, axis=-1, keepdims=True)
    yc = y - mu
    var = jnp.mean(yc * yc, axis=-1, keepdims=True)
    return yc * lax.rsqrt(var + LN_EPS) * g + b


def _dot(a, b):
    return jnp.dot(a, b, preferred_element_type=F32)


def _dot_nt(a, b):
    return lax.dot_general(a, b, (((1,), (1,)), ((), ())), preferred_element_type=F32)


def _split_bf16(x):
    hi = x.astype(BF16)
    lo = (x - hi.astype(F32)).astype(BF16)
    return hi, lo


def _qkv_kernel(x_ref, wk_ref, wqvt_ref, k_ref, qvt_ref):
    xb = x_ref[...].astype(BF16)
    k_ref[...] = _dot(xb, wk_ref[...]).astype(BF16)
    qt = _dot_nt(wqvt_ref[:D_MODEL, :], xb)
    qvt_ref[:D_MODEL, :] = (qt * (MOBA_HEAD_DIM ** -0.5 * LOG2_E)).astype(BF16)
    qvt_ref[D_MODEL:, :] = _dot_nt(wqvt_ref[D_MODEL:, :], xb).astype(BF16)


def _qkv_proj(x, wk_bf16, wqvt_bf16):
    T = x.shape[0]
    return pl.pallas_call(
        _qkv_kernel,
        grid=(T // ROW_TILE,),
        in_specs=[pl.BlockSpec((ROW_TILE, D_MODEL), lambda i: (i, 0)),
                  pl.BlockSpec((D_MODEL, D_MODEL), lambda i: (0, 0)),
                  pl.BlockSpec((2 * D_MODEL, D_MODEL), lambda i: (0, 0))],
        out_specs=[pl.BlockSpec((ROW_TILE, D_MODEL), lambda i: (i, 0)),
                   pl.BlockSpec((2 * D_MODEL, ROW_TILE), lambda i: (0, i))],
        out_shape=[jax.ShapeDtypeStruct((T, D_MODEL), BF16),
                   jax.ShapeDtypeStruct((2 * D_MODEL, T), BF16)],
        compiler_params=_params(("parallel",)),
        name="moba_qkv",
    )(x, wk_bf16, wqvt_bf16)


def _moba_kernel(qt_ref, k_ref, vt_ref, o_ref, s_scr, p_scr, vt_scr, *, n_blocks):
    KB = MOBA_BLOCK
    S = n_blocks * KB
    hd = MOBA_HEAD_DIM
    for h in range(2):
        vt_scr[h, :hd, :] = vt_ref[h * hd:(h + 1) * hd, :]
        vt_scr[h, hd:, :] = jnp.ones((vt_scr.shape[1] - hd, S), BF16)
    qt = qt_ref[...]
    drow = lax.broadcasted_iota(jnp.int32, (LANES, S), 0)
    blk = lax.broadcasted_iota(jnp.int32, (n_blocks, S), 0)
    cur = jnp.right_shift(lax.broadcasted_iota(jnp.int32, (n_blocks, S), 1),
                          KB.bit_length() - 1)
    key_i = lax.broadcasted_iota(jnp.int32, (KB, KB), 0)
    qry_i = lax.broadcasted_iota(jnp.int32, (KB, KB), 1)

    kf = k_ref[...].astype(F32).reshape(n_blocks, KB, LANES)
    kmean = jnp.sum(kf, axis=1) * (1.0 / KB)
    km_hi, km_lo = _split_bf16(kmean)

    qh, bias = [], []
    for h in range(2):
        head_rows = (drow < MOBA_HEAD_DIM) if h == 0 else (drow >= MOBA_HEAD_DIM)
        qh_h = jnp.where(head_rows, qt, jnp.zeros_like(qt))
        bs = _dot(km_hi, qh_h) + _dot(km_lo, qh_h)
        cnt = jnp.zeros((n_blocks, S), jnp.int32)
        for jp in range(n_blocks):
            cj = bs[jp:jp + 1, :]
            beats = (cj > bs) | ((cj == bs) & (jp < blk))
            cnt = cnt + jnp.where(beats & (jp < cur), 1, 0)
        qh.append(qh_h)
        bias.append(jnp.where((blk < cur) & (cnt < MOBA_TOPK), 0.0, NEG_INF))

    steps = [(i, h) for i in range(n_blocks) for h in range(2)]
    n_slots = s_scr.shape[0]
    state = [dict(m=None, acc=None) for _ in steps]

    def block_bias(n, j):
        i, h = steps[n]
        return bias[h][j:j + 1, i * KB:(i + 1) * KB]

    def scores(n):
        i, h = steps[n]
        qi = qh[h][:, i * KB:(i + 1) * KB]
        state[n]["s"] = _dot(k_ref[0:(i + 1) * KB, :], qi)

    def pass1_block(n, j):
        i, _ = steps[n]
        st = state[n]
        s = st["s"][j * KB:(j + 1) * KB, :]
        if j == i:
            s = jnp.where(key_i <= qry_i, s, NEG_INF)
        s_scr[n % n_slots, j * KB:(j + 1) * KB, :] = s
        cm = jnp.max(s, axis=0, keepdims=True)
        if j < i:
            cm = cm + block_bias(n, j)
        st["m"] = cm if st["m"] is None else jnp.maximum(st["m"], cm)

    def pass2_block(n, j):
        i, _ = steps[n]
        m = state[n]["m"]
        shift = -m if j == i else block_bias(n, j) - m
        p = jnp.exp2(s_scr[n % n_slots, j * KB:(j + 1) * KB, :] + shift)
        p_scr[n % n_slots, j * KB:(j + 1) * KB, :] = p.astype(BF16)

    def finish(n):
        i, h = steps[n]
        state[n]["acc"] = _dot(vt_scr[h, :, 0:(i + 1) * KB],
                               p_scr[n % n_slots, 0:(i + 1) * KB, :])
        state[n]["s"] = None
        if h == 1:
            a0, a1 = state[n - 1]["acc"], state[n]["acc"]
            ot = jnp.concatenate([a0[:hd] / a0[hd:hd + 1], a1[:hd] / a1[hd:hd + 1]], axis=0)
            o_ref[i * KB:(i + 1) * KB, :] = ot.T.astype(BF16)

    scores(0)
    for j in range(steps[0][0] + 1):
        pass1_block(0, j)
    for n in range(len(steps)):
        cur_blocks = steps[n][0] + 1
        nxt_blocks = steps[n + 1][0] + 1 if n + 1 < len(steps) else 0
        if nxt_blocks:
            scores(n + 1)
        for j in range(max(cur_blocks, nxt_blocks)):
            if j < nxt_blocks:
                pass1_block(n + 1, j)
            if j < cur_blocks:
                pass2_block(n, j)
        finish(n)


def _moba_attention(k, qvt, batch, seq):
    T = batch * seq
    nb = seq // MOBA_BLOCK
    n_pairs = MOBA_HEADS // 2
    return pl.pallas_call(
        functools.partial(_moba_kernel, n_blocks=nb),
        grid=(batch, n_pairs),
        in_specs=[pl.BlockSpec((LANES, seq), lambda b, hp: (hp, b)),
                  pl.BlockSpec((seq, LANES), lambda b, hp: (b, hp)),
                  pl.BlockSpec((LANES, seq), lambda b, hp: (n_pairs + hp, b))],
        out_specs=pl.BlockSpec((seq, LANES), lambda b, hp: (b, hp)),
        out_shape=jax.ShapeDtypeStruct((T, D_MODEL), BF16),
        scratch_shapes=[pltpu.VMEM((4, seq, MOBA_BLOCK), F32),
                        pltpu.VMEM((4, seq, MOBA_BLOCK), BF16),
                        pltpu.VMEM((2, MOBA_HEAD_DIM + BF16_SUBLANES, seq), BF16)],
        compiler_params=_params(("parallel", "parallel")),
        name="moba_attn",
    )(qvt, k, qvt)


def _proj_ln_kernel(a_ref, w_ref, x_ref, g_ref, b_ref, o_ref):
    y = DEEPNORM_ALPHA * x_ref[...] + _dot(a_ref[...], w_ref[...])
    o_ref[...] = _layer_norm(y, g_ref[...], b_ref[...])


def _proj_ln(a, w_bf16, x, g, b):
    T = x.shape[0]
    row = lambda i: (i, 0)
    fixed = lambda i: (0, 0)
    return pl.pallas_call(
        _proj_ln_kernel,
        grid=(T // ROW_TILE,),
        in_specs=[pl.BlockSpec((ROW_TILE, D_MODEL), row),
                  pl.BlockSpec((D_MODEL, D_MODEL), fixed),
                  pl.BlockSpec((ROW_TILE, D_MODEL), row),
                  pl.BlockSpec((1, D_MODEL), fixed),
                  pl.BlockSpec((1, D_MODEL), fixed)],
        out_specs=pl.BlockSpec((ROW_TILE, D_MODEL), row),
        out_shape=jax.ShapeDtypeStruct((T, D_MODEL), F32),
        compiler_params=_params(("parallel",)),
        name="moba_out_ln",
    )(a, w_bf16, x, g.reshape(1, -1), b.reshape(1, -1))


def _gelu(z):
    return 0.5 * z * (1.0 + lax.erf(z * (2.0 ** -0.5)))


def _gmlp_kernel(x_ref, win_ref, bin_ref, lg_ref, lb_ref, ws_ref, bs_ref, wout_ref,
                 g1_ref, b1_ref, o_ref, u_scr, v_scr, gate_scr):
    tm = x_ref.shape[0]
    cw = GMLP_DV // 4
    xb = x_ref[...].astype(BF16)
    for c in range(8):
        z = _dot(xb, win_ref[:, c * cw:(c + 1) * cw]) + bin_ref[:, c * cw:(c + 1) * cw]
        z = _gelu(z)
        if c < 4:
            u_scr[:, c * cw:(c + 1) * cw] = z.astype(BF16)
        else:
            v_scr[:, (c - 4) * cw:(c - 3) * cw] = z
    vn = _layer_norm(v_scr[...], lg_ref[...], lb_ref[...]).astype(BF16)

    row = lax.broadcasted_iota(jnp.int32, (GMLP_CHUNK, GMLP_CHUNK), 0)
    col = lax.broadcasted_iota(jnp.int32, (GMLP_CHUNK, GMLP_CHUNK), 1)
    gd = GMLP_GROUP_DIM
    for g in range(GMLP_GROUPS):
        wc = jnp.where(col <= row, ws_ref[g], 0.0).astype(BF16)
        bias = bs_ref[:, g:g + 1]
        for c in range(tm // GMLP_CHUNK):
            r0 = c * GMLP_CHUNK
            mixed = _dot(wc, vn[r0:r0 + GMLP_CHUNK, g * gd:(g + 1) * gd]) + bias
            u = u_scr[r0:r0 + GMLP_CHUNK, g * gd:(g + 1) * gd].astype(F32)
            gate_scr[r0:r0 + GMLP_CHUNK, g * gd:(g + 1) * gd] = (u * mixed).astype(BF16)
    y = DEEPNORM_ALPHA * x_ref[...] + _dot(gate_scr[...], wout_ref[...])
    o_ref[...] = _layer_norm(y, g1_ref[...], b1_ref[...])


def _gmlp_layer(x, w_in, b_in, ln_g, ln_b, w_s, b_s_t, w_out, g1, b1):
    T = x.shape[0]
    row = lambda i: (i, 0)
    fixed = lambda i: (0, 0)
    once = pl.Buffered(1)
    return pl.pallas_call(
        _gmlp_kernel,
        grid=(T // ROW_TILE,),
        in_specs=[pl.BlockSpec((ROW_TILE, D_MODEL), row),
                  pl.BlockSpec((D_MODEL, 2 * GMLP_DV), fixed, pipeline_mode=once),
                  pl.BlockSpec((1, 2 * GMLP_DV), fixed),
                  pl.BlockSpec((1, GMLP_DV), fixed),
                  pl.BlockSpec((1, GMLP_DV), fixed),
                  pl.BlockSpec((GMLP_GROUPS, GMLP_CHUNK, GMLP_CHUNK), lambda i: (0, 0, 0)),
                  pl.BlockSpec((GMLP_CHUNK, GMLP_GROUPS), fixed),
                  pl.BlockSpec((GMLP_DV, D_MODEL), fixed, pipeline_mode=once),
                  pl.BlockSpec((1, D_MODEL), fixed),
                  pl.BlockSpec((1, D_MODEL), fixed)],
        out_specs=pl.BlockSpec((ROW_TILE, D_MODEL), row),
        out_shape=jax.ShapeDtypeStruct((T, D_MODEL), F32),
        scratch_shapes=[pltpu.VMEM((ROW_TILE, GMLP_DV), BF16),
                        pltpu.VMEM((ROW_TILE, GMLP_DV), F32),
                        pltpu.VMEM((ROW_TILE, GMLP_DV), BF16)],
        compiler_params=_params(("parallel",)),
        name="gmlp",
    )(x, w_in, b_in.reshape(1, -1), ln_g.reshape(1, -1), ln_b.reshape(1, -1), w_s, b_s_t,
      w_out, g1.reshape(1, -1), b1.reshape(1, -1))


def _router_kernel(x_ref, w_ref, b_ref, ids_ref, meta_ref, cnt_ref, carry):
    tm = x_ref.shape[0]

    @pl.when(pl.program_id(0) == 0)
    def _():
        carry[...] = jnp.zeros_like(carry)

    xh, xl = _split_bf16(x_ref[...])
    wh, wl = _split_bf16(w_ref[...])
    hi = _dot(xh, jnp.concatenate([wh, wl], axis=1))
    logits = (hi[:, :LANES] + hi[:, LANES:]) + _dot(xl, wh) + b_ref[...]
    lane_i = lax.broadcasted_iota(jnp.int32, (tm, LANES), 1)
    lane = lane_i.astype(F32)

    def first_max(vals):
        top = jnp.max(vals, axis=-1, keepdims=True)
        idx = jnp.min(jnp.where(vals == top, lane, float(LANES)), axis=-1, keepdims=True)
        return top, idx

    gl = jnp.where(lane_i < ROUTER_EXPERT_LANE0, logits, NEG_INF)
    g_top, g_idx = first_max(gl)
    g_p = 1.0 / jnp.sum(jnp.exp(gl - g_top), axis=-1, keepdims=True)
    lo = ROUTER_EXPERT_LANE0 + g_idx * MOE_EXPERTS_PER_GROUP
    el = jnp.where((lane >= lo) & (lane < lo + MOE_EXPERTS_PER_GROUP), logits, NEG_INF)
    e1, i1 = first_max(el)
    e2, i2 = first_max(jnp.where(lane == i1, NEG_INF, el))
    d = jnp.exp(e2 - e1)
    gate1 = g_p / (1.0 + d)
    gate2 = g_p * d / (1.0 + d)

    r = lax.broadcasted_iota(jnp.int32, (tm, tm), 0)
    c = lax.broadcasted_iota(jnp.int32, (tm, tm), 1)
    before = jnp.where(c < r, 1.0, 0.0).astype(BF16)
    oh = lane == g_idx
    ohf = jnp.where(oh, 1.0, 0.0)
    ahead = _dot(before, ohf.astype(BF16))
    base = carry[...]
    rank = jnp.sum(jnp.where(oh, base + ahead, 0.0), axis=-1, keepdims=True)
    total = base + jnp.sum(ohf, axis=0, keepdims=True)
    carry[...] = total
    cnt_ref[...] = jnp.broadcast_to(total, cnt_ref.shape)

    id_cols = jnp.where(lane_i == 0, g_idx, jnp.where(lane_i == 1, rank, 0.0))
    ids_ref[...] = id_cols.T[:ROUTER_ID_ROWS, :].astype(jnp.int32)
    meta_ref[...] = jnp.where(lane_i == META_E1, i1 - lo,
                    jnp.where(lane_i == META_E2, i2 - lo,
                    jnp.where(lane_i == META_G1, gate1,
                    jnp.where(lane_i == META_G2, gate2, 0.0))))


def _router(x, w_pad, b_pad):
    T = x.shape[0]
    row = lambda i: (i, 0)
    fixed = lambda i: (0, 0)
    return pl.pallas_call(
        _router_kernel,
        grid=(T // ROW_TILE,),
        in_specs=[pl.BlockSpec((ROW_TILE, D_MODEL), row),
                  pl.BlockSpec((D_MODEL, LANES), fixed),
                  pl.BlockSpec((1, LANES), fixed)],
        out_specs=[pl.BlockSpec((ROUTER_ID_ROWS, ROW_TILE), lambda i: (0, i)),
                   pl.BlockSpec((ROW_TILE, LANES), row),
                   pl.BlockSpec((8, LANES), fixed)],
        out_shape=[jax.ShapeDtypeStruct((ROUTER_ID_ROWS, T), jnp.int32),
                   jax.ShapeDtypeStruct((T, LANES), F32),
                   jax.ShapeDtypeStruct((8, LANES), F32)],
        scratch_shapes=[pltpu.VMEM((1, LANES), F32)],
        compiler_params=_params(("arbitrary",)),
        name="moe_router",
    )(x, w_pad, b_pad)


def _tile_row(ref, row, sublanes):
    return ref.at[pl.ds(pl.multiple_of(row * sublanes, sublanes), sublanes)]


def _tile_row_copy(src_ref, src_row, dst_ref, dst_row, sem, sublanes):
    return pltpu.make_async_copy(_tile_row(src_ref, src_row, sublanes),
                                 _tile_row(dst_ref, dst_row, sublanes), sem)


def _tile_rows_wait(src_ref, dst_ref, n_rows, sem, sublanes):
    n = n_rows * sublanes
    pltpu.make_async_copy(src_ref.at[pl.ds(0, n)], dst_ref.at[pl.ds(0, n)], sem).wait()


def _load_tile_rows(ref, n_rows, sublanes):
    return jnp.concatenate([ref[pl.ds(c, n_rows, stride=sublanes), :]
                            for c in range(ROW_SUBLANES)], axis=1)


def _store_tile_rows(ref, val, sublanes):
    for c in range(ROW_SUBLANES):
        ref[pl.ds(c, val.shape[0], stride=sublanes), :] = val[:, c * LANES:(c + 1) * LANES]


def _dispatch_kernel(dest_ref, zero_ref, x_ref, meta_ref, buf_hbm, row_scr, sem):
    tm = x_ref.shape[0]
    ts = TOKEN_SUBLANES
    i = pl.program_id(0)
    n = pl.num_programs(0)
    slot = i % 2

    @pl.when(i == 0)
    def _():
        row_scr[...] = jnp.zeros_like(row_scr)

        def zero_block(b, start):
            @pl.when(zero_ref[b] != 0)
            def _():
                for k in range(MOE_ROWS // tm):
                    rows = pl.ds((b * MOE_ROWS + k * tm) * ts, tm * ts)
                    cp = pltpu.make_async_copy(row_scr.at[1], buf_hbm.at[rows], sem.at[1])
                    if start:
                        cp.start()
                    else:
                        cp.wait()
        for b in range(zero_ref.shape[0]):
            zero_block(b, True)
        for b in range(zero_ref.shape[0]):
            zero_block(b, False)

    stage = row_scr.at[slot]

    @pl.when(i >= 2)
    def _():
        _tile_rows_wait(stage, buf_hbm, tm, sem.at[slot], ts)

    _store_tile_rows(stage, x_ref[...], ts)
    stage[pl.ds(ROW_SUBLANES, tm, stride=ts), :] = meta_ref[...]

    def issue(r, _):
        _tile_row_copy(stage, r, buf_hbm, dest_ref[i * tm + r], sem.at[slot], ts).start()
        return 0
    lax.fori_loop(0, tm, issue, 0, unroll=ISSUE_UNROLL)

    @pl.when(i == n - 1)
    def _():
        _tile_rows_wait(stage, buf_hbm, tm, sem.at[slot], ts)

        @pl.when(n >= 2)
        def _():
            _tile_rows_wait(row_scr.at[1 - slot], buf_hbm, tm, sem.at[1 - slot], ts)


def _dispatch(dest, zero_flag, x, meta, n_rows):
    T = x.shape[0]
    return pl.pallas_call(
        _dispatch_kernel,
        grid_spec=pltpu.PrefetchScalarGridSpec(
            num_scalar_prefetch=2,
            grid=(T // COMBINE_TILE,),
            in_specs=[pl.BlockSpec((COMBINE_TILE, D_MODEL), lambda i, d, z: (i, 0)),
                      pl.BlockSpec((COMBINE_TILE, LANES), lambda i, d, z: (i, 0))],
            out_specs=pl.BlockSpec(memory_space=pl.ANY),
            scratch_shapes=[pltpu.VMEM((2, COMBINE_TILE * TOKEN_SUBLANES, LANES), F32),
                            pltpu.SemaphoreType.DMA((2,))]),
        out_shape=jax.ShapeDtypeStruct((n_rows * TOKEN_SUBLANES, LANES), F32),
        compiler_params=pltpu.CompilerParams(dimension_semantics=("arbitrary",),
                                             has_side_effects=True),
        name="moe_dispatch",
    )(dest, zero_flag, x, meta)

def _ffn_kernel(bg_ref, nused_ref, nxt_ref, nvalid_ref, rows_ref, w1_hbm, w3_hbm, w2_hbm, o_ref,
                w1_f32, w3_f32, w2_f32, w1_ref, w3_ref, w2_ref, sem, *, layer):
    i = pl.program_id(0)
    used = i < nused_ref[0]
    epg = MOE_EXPERTS_PER_GROUP

    def fetch(group):
        first = pl.multiple_of(group * epg, epg)
        return [pltpu.make_async_copy(src.at[layer, pl.ds(first, epg)], dst, sem.at[k])
                for k, (src, dst) in enumerate(((w1_hbm, w1_f32), (w3_hbm, w3_f32),
                                                (w2_hbm, w2_f32)))]

    @pl.when(i == 0)
    def _():
        for cp in fetch(bg_ref[0]):
            cp.start()

    @pl.when(used & ((i == 0) | (bg_ref[i] != bg_ref[jnp.maximum(i - 1, 0)])))
    def _():
        for cp in fetch(bg_ref[i]):
            cp.wait()
        for e in range(epg):
            w1_ref[e] = w1_f32[e].astype(BF16)
            w3_ref[e] = w3_f32[e].astype(BF16)
            w2_ref[e] = w2_f32[e].astype(BF16)

        @pl.when(nxt_ref[i] >= 0)
        def _():
            for cp in fetch(nxt_ref[i]):
                cp.start()

    def compute(n_rows):
        x = _load_tile_rows(rows_ref, n_rows, TOKEN_SUBLANES).astype(BF16)
        meta = rows_ref[pl.ds(ROW_SUBLANES, n_rows, stride=TOKEN_SUBLANES), :]
        e1 = meta[:, META_E1:META_E1 + 1]
        e2 = meta[:, META_E2:META_E2 + 1]
        g1 = meta[:, META_G1:META_G1 + 1]
        g2 = meta[:, META_G2:META_G2 + 1]
        acc = None
        for e in range(epg):
            h1 = _dot(x, w1_ref[e])
            h3 = _dot(x, w3_ref[e])
            gate = jnp.where(e1 == e, g1, g2)
            h = h1 * (1.0 / (1.0 + jnp.exp(-h1))) * h3 * gate
            h = jnp.where((e1 == e) | (e2 == e), h, 0.0).astype(BF16)
            y = _dot(h, w2_ref[e])
            acc = y if acc is None else acc + y
        _store_tile_rows(o_ref, acc, ROW_SUBLANES)
        if n_rows < MOE_ROWS:
            rest = (MOE_ROWS - n_rows) * ROW_SUBLANES
            o_ref[pl.ds(n_rows * ROW_SUBLANES, rest), :] = jnp.zeros((rest, LANES), F32)

    half_full = nvalid_ref[i] <= MOE_ROWS // 2

    @pl.when(used & jnp.logical_not(half_full))
    def _():
        compute(MOE_ROWS)

    @pl.when(used & half_full)
    def _():
        compute(MOE_ROWS // 2)

    @pl.when(jnp.logical_not(used))
    def _():
        o_ref[...] = jnp.zeros_like(o_ref)


def _expert_ffn(blk_group, n_used, next_group, n_valid, buf, w1, w3, w2, layer):
    n_rows = buf.shape[0] // TOKEN_SUBLANES
    epg = MOE_EXPERTS_PER_GROUP
    up = (epg, D_MODEL, MOE_D_EXPERT)
    down = (epg, MOE_D_EXPERT, D_MODEL)
    return pl.pallas_call(
        functools.partial(_ffn_kernel, layer=layer),
        grid_spec=pltpu.PrefetchScalarGridSpec(
            num_scalar_prefetch=4,
            grid=(n_rows // MOE_ROWS,),
            in_specs=[pl.BlockSpec((MOE_ROWS * TOKEN_SUBLANES, LANES),
                                   lambda i, bg, nu, nx, nv: (jnp.minimum(i, nu[0] - 1), 0)),
                      pl.BlockSpec(memory_space=pl.ANY),
                      pl.BlockSpec(memory_space=pl.ANY),
                      pl.BlockSpec(memory_space=pl.ANY)],
            out_specs=pl.BlockSpec((MOE_ROWS * ROW_SUBLANES, LANES),
                                   lambda i, bg, nu, nx, nv: (i, 0)),
            scratch_shapes=[pltpu.VMEM(up, F32), pltpu.VMEM(up, F32), pltpu.VMEM(down, F32),
                            pltpu.VMEM(up, BF16), pltpu.VMEM(up, BF16), pltpu.VMEM(down, BF16),
                            pltpu.SemaphoreType.DMA((3,))]),
        out_shape=jax.ShapeDtypeStruct((n_rows * ROW_SUBLANES, LANES), F32),
        compiler_params=_params(("arbitrary",), vmem=FFN_VMEM_LIMIT),
        name="moe_ffn",
    )(blk_group, n_used, next_group, n_valid, buf, w1, w3, w2)


def _combine_kernel(dest_ref, y_hbm, x_ref, g_ref, b_ref, o_ref, y_scr, sem):
    tm = x_ref.shape[0]
    i = pl.program_id(0)

    def gather(tile, slot):
        for r in range(tm):
            _tile_row_copy(y_hbm, dest_ref[tile * tm + r], y_scr.at[slot], r, sem.at[slot],
                           ROW_SUBLANES).start()

    @pl.when(i == 0)
    def _():
        gather(0, 0)

    @pl.when(i + 1 < pl.num_programs(0))
    def _():
        gather(i + 1, (i + 1) % 2)

    slot = i % 2
    _tile_rows_wait(y_hbm, y_scr.at[slot], tm, sem.at[slot], ROW_SUBLANES)
    y = DEEPNORM_ALPHA * x_ref[...] + _load_tile_rows(y_scr.at[slot], tm, ROW_SUBLANES)
    o_ref[...] = _layer_norm(y, g_ref[...], b_ref[...])


def _combine_ln(dest, y_buf, x, g, b):
    T = x.shape[0]
    row = lambda i, d: (i, 0)
    fixed = lambda i, d: (0, 0)
    return pl.pallas_call(
        _combine_kernel,
        grid_spec=pltpu.PrefetchScalarGridSpec(
            num_scalar_prefetch=1,
            grid=(T // COMBINE_TILE,),
            in_specs=[pl.BlockSpec(memory_space=pl.ANY),
                      pl.BlockSpec((COMBINE_TILE, D_MODEL), row),
                      pl.BlockSpec((1, D_MODEL), fixed),
                      pl.BlockSpec((1, D_MODEL), fixed)],
            out_specs=pl.BlockSpec((COMBINE_TILE, D_MODEL), row),
            scratch_shapes=[pltpu.VMEM((2, COMBINE_TILE * ROW_SUBLANES, LANES), F32),
                            pltpu.SemaphoreType.DMA((2,))]),
        out_shape=jax.ShapeDtypeStruct((T, D_MODEL), F32),
        compiler_params=_params(("arbitrary",)),
        name="moe_combine_ln",
    )(dest, y_buf, x, g.reshape(1, -1), b.reshape(1, -1))


def _hier_moe_ln(x, w_grp, b_grp, w_rt, b_rt, w1, w3, w2, layer, g2, b2):
    T = x.shape[0]
    pad_w = LANES - MOE_GROUPS - MOE_EXPERTS
    w_pad = jnp.concatenate([w_grp, w_rt, jnp.zeros((D_MODEL, pad_w), F32)], axis=1)
    b_pad = jnp.concatenate([b_grp, b_rt, jnp.zeros((pad_w,), F32)]).reshape(1, LANES)
    ids, meta, cnt = _router(x, w_pad, b_pad)

    counts = cnt[0, ROUTER_GROUP_LANE0:ROUTER_GROUP_LANE0 + MOE_GROUPS].astype(jnp.int32)
    padded = ((counts + MOE_ROWS - 1) // MOE_ROWS) * MOE_ROWS
    pad_end = jnp.cumsum(padded)
    pad_start = pad_end - padded
    dest = ids[1]
    for g in range(MOE_GROUPS):
        dest = dest + jnp.where(ids[0] == g, pad_start[g], 0)
    n_blk = T // MOE_ROWS + MOE_GROUPS
    n_used = (pad_end[-1:] // MOE_ROWS).astype(jnp.int32)
    blk_row0 = jnp.minimum(jnp.arange(n_blk), n_used - 1) * MOE_ROWS
    blk_group = jnp.sum(pad_end[None, :] <= blk_row0[:, None], axis=1).astype(jnp.int32)
    blk_end = (jnp.arange(n_blk) + 1) * MOE_ROWS
    group_last = jnp.any((blk_end[:, None] == pad_end[None, :]) & (padded[None, :] > 0), axis=1)
    zero_flag = (group_last | (jnp.arange(n_blk) >= n_used)).astype(jnp.int32)

    after = pad_end[blk_group] // MOE_ROWS
    next_group = jnp.where(after < n_used, blk_group[jnp.minimum(after, n_blk - 1)], -1)
    next_group = next_group.astype(jnp.int32)

    buf = _dispatch(dest, zero_flag, x, meta, n_blk * MOE_ROWS)
    n_valid = jnp.clip((pad_start + counts)[blk_group] - jnp.arange(n_blk) * MOE_ROWS,
                       0, MOE_ROWS).astype(jnp.int32)
    y_buf = _expert_ffn(blk_group, n_used, next_group, n_valid, buf, w1, w3, w2, layer)
    return _combine_ln(dest, y_buf, x, g2, b2)


def kernel(x, moba_w_qkv, moba_w_o, gmlp_w_in, gmlp_b_in, gmlp_ln_g, gmlp_ln_b, gmlp_w_s,
           gmlp_b_s, gmlp_w_out, ln1_g, ln1_b, ln2_g, ln2_b, moe_w_grp, moe_b_grp, moe_w_rt,
           moe_b_rt, moe_w1, moe_w3, moe_w2):
    B, S, D = x.shape
    assert D == D_MODEL and S % MOBA_BLOCK == 0 and (B * S) % ROW_TILE == 0
    xf = x.reshape(B * S, D)
    for i in range(DEPTH):
        j = i // 2
        if i % 2 == 0:
            wk = moba_w_qkv[j, :, D:2 * D].astype(BF16)
            wqvt = jnp.concatenate([moba_w_qkv[j, :, :D], moba_w_qkv[j, :, 2 * D:]],
                                   axis=1).T.astype(BF16)
            k, qvt = _qkv_proj(xf, wk, wqvt)
            att = _moba_attention(k, qvt, B, S)
            xf = _proj_ln(att, moba_w_o[j].astype(BF16), xf, ln1_g[i], ln1_b[i])
        else:
            xf = _gmlp_layer(xf, gmlp_w_in[j].astype(BF16), gmlp_b_in[j], gmlp_ln_g[j],
                             gmlp_ln_b[j], gmlp_w_s[j], gmlp_b_s[j].T,
                             gmlp_w_out[j].astype(BF16), ln1_g[i], ln1_b[i])
        xf = _hier_moe_ln(xf, moe_w_grp[i], moe_b_grp[i], moe_w_rt[i], moe_b_rt[i],
                          moe_w1, moe_w3, moe_w2, i, ln2_g[i], ln2_b[i])
    return xf.reshape(B, S, D)
```

```python
import functools

import jax
import jax.numpy as jnp
from jax import lax
from jax.experimental import pallas as pl
from jax.experimental.pallas import tpu as pltpu

F32 = jnp.float32
BF16 = jnp.bfloat16

D_MODEL = 1024
DEPTH = 4
MOBA_HEADS = 16
MOBA_HEAD_DIM = 64
MOBA_BLOCK = 256
MOBA_TOPK = 3
GMLP_DV = 3072
GMLP_GROUPS = 8
GMLP_GROUP_DIM = GMLP_DV // GMLP_GROUPS
GMLP_CHUNK = 128
MOE_GROUPS = 8
MOE_EXPERTS_PER_GROUP = 8
MOE_EXPERTS = 64
MOE_D_EXPERT = 256
LN_EPS = 1e-5
DEEPNORM_ALPHA = (2.0 * DEPTH) ** 0.25

LANES = 128
BF16_SUBLANES = 16
ROW_SUBLANES = D_MODEL // LANES
TOKEN_SUBLANES = 2 * ROW_SUBLANES
LOG2_E = 1.4426950408889634
ROUTER_GROUP_LANE0 = 0
ROUTER_EXPERT_LANE0 = 8
ROUTER_ID_ROWS = 8
MOE_ROWS = 512
META_E1, META_E2, META_G1, META_G2 = 0, 1, 2, 3
ROW_TILE = 512
COMBINE_TILE = 512
ISSUE_UNROLL = 8
DMA_PRIORITIES = 2
VMEM_LIMIT = 56 * 1024 * 1024
FFN_VMEM_LIMIT = 62 * 1024 * 1024
NEG_INF = float("-inf")


def _params(sem, vmem=VMEM_LIMIT):
    return pltpu.CompilerParams(dimension_semantics=sem, vmem_limit_bytes=vmem)


def _layer_norm(y, g, b):
    mu = jnp.mean(y, axis=-1, keepdims=True)
    yc = y - mu
    var = jnp.mean(yc * yc, axis=-1, keepdims=True)
    return yc * lax.rsqrt(var + LN_EPS) * g + b


def _dot(a, b):
    return jnp.dot(a, b, preferred_element_type=F32)


def _dot_nt(a, b):
    return lax.dot_general(a, b, (((1,), (1,)), ((), ())), preferred_element_type=F32)


def _split_bf16(x):
    hi = x.astype(BF16)
    lo = (x - hi.astype(F32)).astype(BF16)
    return hi, lo


def _qkv_kernel(x_ref, wk_ref, wqvt_ref, k_ref, qvt_ref):
    xb = x_ref[...].astype(BF16)
    k_ref[...] = _dot(xb, wk_ref[...]).astype(BF16)
    qt = _dot_nt(wqvt_ref[:D_MODEL, :], xb)
    qvt_ref[:D_MODEL, :] = (qt * (MOBA_HEAD_DIM ** -0.5 * LOG2_E)).astype(BF16)
    qvt_ref[D_MODEL:, :] = _dot_nt(wqvt_ref[D_MODEL:, :], xb).astype(BF16)


def _qkv_proj(x, wk_bf16, wqvt_bf16):
    T = x.shape[0]
    return pl.pallas_call(
        _qkv_kernel,
        grid=(T // ROW_TILE,),
        in_specs=[pl.BlockSpec((ROW_TILE, D_MODEL), lambda i: (i, 0)),
                  pl.BlockSpec((D_MODEL, D_MODEL), lambda i: (0, 0)),
                  pl.BlockSpec((2 * D_MODEL, D_MODEL), lambda i: (0, 0))],
        out_specs=[pl.BlockSpec((ROW_TILE, D_MODEL), lambda i: (i, 0)),
                   pl.BlockSpec((2 * D_MODEL, ROW_TILE), lambda i: (0, i))],
        out_shape=[jax.ShapeDtypeStruct((T, D_MODEL), BF16),
                   jax.ShapeDtypeStruct((2 * D_MODEL, T), BF16)],
        compiler_params=_params(("parallel",)),
        name="moba_qkv",
    )(x, wk_bf16, wqvt_bf16)


def _moba_kernel(qt_ref, k_ref, vt_ref, o_ref, s_scr, p_scr, vt_scr, *, n_blocks):
    KB = MOBA_BLOCK
    S = n_blocks * KB
    hd = MOBA_HEAD_DIM
    for h in range(2):
        vt_scr[h, :hd, :] = vt_ref[h * hd:(h + 1) * hd, :]
        vt_scr[h, hd:, :] = jnp.ones((vt_scr.shape[1] - hd, S), BF16)
    qt = qt_ref[...]
    drow = lax.broadcasted_iota(jnp.int32, (LANES, S), 0)
    blk = lax.broadcasted_iota(jnp.int32, (n_blocks, S), 0)
    cur = jnp.right_shift(lax.broadcasted_iota(jnp.int32, (n_blocks, S), 1),
                          KB.bit_length() - 1)
    key_i = lax.broadcasted_iota(jnp.int32, (KB, KB), 0)
    qry_i = lax.broadcasted_iota(jnp.int32, (KB, KB), 1)

    kf = k_ref[...].astype(F32).reshape(n_blocks, KB, LANES)
    kmean = jnp.sum(kf, axis=1) * (1.0 / KB)
    km_hi, km_lo = _split_bf16(kmean)

    qh, bias = [], []
    for h in range(2):
        head_rows = (drow < MOBA_HEAD_DIM) if h == 0 else (drow >= MOBA_HEAD_DIM)
        qh_h = jnp.where(head_rows, qt, jnp.zeros_like(qt))
        bs = _dot(km_hi, qh_h) + _dot(km_lo, qh_h)
        cnt = jnp.zeros((n_blocks, S), jnp.int32)
        for jp in range(n_blocks):
            cj = bs[jp:jp + 1, :]
            beats = (cj > bs) | ((cj == bs) & (jp < blk))
            cnt = cnt + jnp.where(beats & (jp < cur), 1, 0)
        qh.append(qh_h)
        bias.append(jnp.where((blk < cur) & (cnt < MOBA_TOPK), 0.0, NEG_INF))

    steps = [(i, h) for i in range(n_blocks) for h in range(2)]
    n_slots = s_scr.shape[0]
    state = [dict(m=None, acc=None) for _ in steps]

    def block_bias(n, j):
        i, h = steps[n]
        return bias[h][j:j + 1, i * KB:(i + 1) * KB]

    def scores(n):
        i, h = steps[n]
        qi = qh[h][:, i * KB:(i + 1) * KB]
        state[n]["s"] = _dot(k_ref[0:(i + 1) * KB, :], qi)

    def pass1_block(n, j):
        i, _ = steps[n]
        st = state[n]
        s = st["s"][j * KB:(j + 1) * KB, :]
        if j == i:
            s = jnp.where(key_i <= qry_i, s, NEG_INF)
        s_scr[n % n_slots, j * KB:(j + 1) * KB, :] = s
        cm = jnp.max(s, axis=0, keepdims=True)
        if j < i:
            cm = cm + block_bias(n, j)
        st["m"] = cm if st["m"] is None else jnp.maximum(st["m"], cm)

    def pass2_block(n, j):
        i, _ = steps[n]
        m = state[n]["m"]
        shift = -m if j == i else block_bias(n, j) - m
        p = jnp.exp2(s_scr[n % n_slots, j * KB:(j + 1) * KB, :] + shift)
        p_scr[n % n_slots, j * KB:(j + 1) * KB, :] = p.astype(BF16)

    def finish(n):
        i, h = steps[n]
        state[n]["acc"] = _dot(vt_scr[h, :, 0:(i + 1) * KB],
                               p_scr[n % n_slots, 0:(i + 1) * KB, :])
        state[n]["s"] = None
        if h == 1:
            a0, a1 = state[n - 1]["acc"], state[n]["acc"]
            ot = jnp.concatenate([a0[:hd] / a0[hd:hd + 1], a1[:hd] / a1[hd:hd + 1]], axis=0)
            o_ref[i * KB:(i + 1) * KB, :] = ot.T.astype(BF16)

    scores(0)
    for j in range(steps[0][0] + 1):
        pass1_block(0, j)
    for n in range(len(steps)):
        cur_blocks = steps[n][0] + 1
        nxt_blocks = steps[n + 1][0] + 1 if n + 1 < len(steps) else 0
        if nxt_blocks:
            scores(n + 1)
        for j in range(max(cur_blocks, nxt_blocks)):
            if j < nxt_blocks:
                pass1_block(n + 1, j)
            if j < cur_blocks:
                pass2_block(n, j)
        finish(n)


def _moba_attention(k, qvt, batch, seq):
    T = batch * seq
    nb = seq // MOBA_BLOCK
    n_pairs = MOBA_HEADS // 2
    return pl.pallas_call(
        functools.partial(_moba_kernel, n_blocks=nb),
        grid=(batch, n_pairs),
        in_specs=[pl.BlockSpec((LANES, seq), lambda b, hp: (hp, b)),
                  pl.BlockSpec((seq, LANES), lambda b, hp: (b, hp)),
                  pl.BlockSpec((LANES, seq), lambda b, hp: (n_pairs + hp, b))],
        out_specs=pl.BlockSpec((seq, LANES), lambda b, hp: (b, hp)),
        out_shape=jax.ShapeDtypeStruct((T, D_MODEL), BF16),
        scratch_shapes=[pltpu.VMEM((4, seq, MOBA_BLOCK), F32),
                        pltpu.VMEM((4, seq, MOBA_BLOCK), BF16),
                        pltpu.VMEM((2, MOBA_HEAD_DIM + BF16_SUBLANES, seq), BF16)],
        compiler_params=_params(("parallel", "parallel")),
        name="moba_attn",
    )(qvt, k, qvt)


def _proj_ln_kernel(a_ref, w_ref, x_ref, g_ref, b_ref, o_ref):
    y = DEEPNORM_ALPHA * x_ref[...] + _dot(a_ref[...], w_ref[...])
    o_ref[...] = _layer_norm(y, g_ref[...], b_ref[...])


def _proj_ln(a, w_bf16, x, g, b):
    T = x.shape[0]
    row = lambda i: (i, 0)
    fixed = lambda i: (0, 0)
    return pl.pallas_call(
        _proj_ln_kernel,
        grid=(T // ROW_TILE,),
        in_specs=[pl.BlockSpec((ROW_TILE, D_MODEL), row),
                  pl.BlockSpec((D_MODEL, D_MODEL), fixed),
                  pl.BlockSpec((ROW_TILE, D_MODEL), row),
                  pl.BlockSpec((1, D_MODEL), fixed),
                  pl.BlockSpec((1, D_MODEL), fixed)],
        out_specs=pl.BlockSpec((ROW_TILE, D_MODEL), row),
        out_shape=jax.ShapeDtypeStruct((T, D_MODEL), F32),
        compiler_params=_params(("parallel",)),
        name="moba_out_ln",
    )(a, w_bf16, x, g.reshape(1, -1), b.reshape(1, -1))


def _gelu(z):
    return 0.5 * z * (1.0 + lax.erf(z * (2.0 ** -0.5)))


def _gmlp_kernel(x_ref, win_ref, bin_ref, lg_ref, lb_ref, ws_ref, bs_ref, wout_ref,
                 g1_ref, b1_ref, o_ref, u_scr, v_scr, gate_scr):
    tm = x_ref.shape[0]
    cw = GMLP_DV // 4
    xb = x_ref[...].astype(BF16)
    for c in range(8):
        z = _dot(xb, win_ref[:, c * cw:(c + 1) * cw]) + bin_ref[:, c * cw:(c + 1) * cw]
        z = _gelu(z)
        if c < 4:
            u_scr[:, c * cw:(c + 1) * cw] = z.astype(BF16)
        else:
            v_scr[:, (c - 4) * cw:(c - 3) * cw] = z
    vn = _layer_norm(v_scr[...], lg_ref[...], lb_ref[...]).astype(BF16)

    row = lax.broadcasted_iota(jnp.int32, (GMLP_CHUNK, GMLP_CHUNK), 0)
    col = lax.broadcasted_iota(jnp.int32, (GMLP_CHUNK, GMLP_CHUNK), 1)
    gd = GMLP_GROUP_DIM
    for g in range(GMLP_GROUPS):
        wc = jnp.where(col <= row, ws_ref[g], 0.0).astype(BF16)
        bias = bs_ref[:, g:g + 1]
        for c in range(tm // GMLP_CHUNK):
            r0 = c * GMLP_CHUNK
            mixed = _dot(wc, vn[r0:r0 + GMLP_CHUNK, g * gd:(g + 1) * gd]) + bias
            u = u_scr[r0:r0 + GMLP_CHUNK, g * gd:(g + 1) * gd].astype(F32)
            gate_scr[r0:r0 + GMLP_CHUNK, g * gd:(g + 1) * gd] = (u * mixed).astype(BF16)
    y = DEEPNORM_ALPHA * x_ref[...] + _dot(gate_scr[...], wout_ref[...])
    o_ref[...] = _layer_norm(y, g1_ref[...], b1_ref[...])


def _gmlp_layer(x, w_in, b_in, ln_g, ln_b, w_s, b_s_t, w_out, g1, b1):
    T = x.shape[0]
    row = lambda i: (i, 0)
    fixed = lambda i: (0, 0)
    once = pl.Buffered(1)
    return pl.pallas_call(
        _gmlp_kernel,
        grid=(T // ROW_TILE,),
        in_specs=[pl.BlockSpec((ROW_TILE, D_MODEL), row),
                  pl.BlockSpec((D_MODEL, 2 * GMLP_DV), fixed, pipeline_mode=once),
                  pl.BlockSpec((1, 2 * GMLP_DV), fixed),
                  pl.BlockSpec((1, GMLP_DV), fixed),
                  pl.BlockSpec((1, GMLP_DV), fixed),
                  pl.BlockSpec((GMLP_GROUPS, GMLP_CHUNK, GMLP_CHUNK), lambda i: (0, 0, 0)),
                  pl.BlockSpec((GMLP_CHUNK, GMLP_GROUPS), fixed),
                  pl.BlockSpec((GMLP_DV, D_MODEL), fixed, pipeline_mode=once),
                  pl.BlockSpec((1, D_MODEL), fixed),
                  pl.BlockSpec((1, D_MODEL), fixed)],
        out_specs=pl.BlockSpec((ROW_TILE, D_MODEL), row),
        out_shape=jax.ShapeDtypeStruct((T, D_MODEL), F32),
        scratch_shapes=[pltpu.VMEM((ROW_TILE, GMLP_DV), BF16),
                        pltpu.VMEM((ROW_TILE, GMLP_DV), F32),
                        pltpu.VMEM((ROW_TILE, GMLP_DV), BF16)],
        compiler_params=_params(("parallel",)),
        name="gmlp",
    )(x, w_in, b_in.reshape(1, -1), ln_g.reshape(1, -1), ln_b.reshape(1, -1), w_s, b_s_t,
      w_out, g1.reshape(1, -1), b1.reshape(1, -1))


def _router_kernel(x_ref, w_ref, b_ref, ids_ref, meta_ref, cnt_ref, carry):
    tm = x_ref.shape[0]

    @pl.when(pl.program_id(0) == 0)
    def _():
        carry[...] = jnp.zeros_like(carry)

    xh, xl = _split_bf16(x_ref[...])
    wh, wl = _split_bf16(w_ref[...])
    hi = _dot(xh, jnp.concatenate([wh, wl], axis=1))
    logits = (hi[:, :LANES] + hi[:, LANES:]) + _dot(xl, wh) + b_ref[...]
    lane_i = lax.broadcasted_iota(jnp.int32, (tm, LANES), 1)
    lane = lane_i.astype(F32)

    def first_max(vals):
        top = jnp.max(vals, axis=-1, keepdims=True)
        idx = jnp.min(jnp.where(vals == top, lane, float(LANES)), axis=-1, keepdims=True)
        return top, idx

    gl = jnp.where(lane_i < ROUTER_EXPERT_LANE0, logits, NEG_INF)
    g_top, g_idx = first_max(gl)
    g_p = 1.0 / jnp.sum(jnp.exp(gl - g_top), axis=-1, keepdims=True)
    lo = ROUTER_EXPERT_LANE0 + g_idx * MOE_EXPERTS_PER_GROUP
    el = jnp.where((lane >= lo) & (lane < lo + MOE_EXPERTS_PER_GROUP), logits, NEG_INF)
    e1, i1 = first_max(el)
    e2, i2 = first_max(jnp.where(lane == i1, NEG_INF, el))
    d = jnp.exp(e2 - e1)
    gate1 = g_p / (1.0 + d)
    gate2 = g_p * d / (1.0 + d)

    r = lax.broadcasted_iota(jnp.int32, (tm, tm), 0)
    c = lax.broadcasted_iota(jnp.int32, (tm, tm), 1)
    before = jnp.where(c < r, 1.0, 0.0).astype(BF16)
    oh = lane == g_idx
    ohf = jnp.where(oh, 1.0, 0.0)
    ahead = _dot(before, ohf.astype(BF16))
    base = carry[...]
    rank = jnp.sum(jnp.where(oh, base + ahead, 0.0), axis=-1, keepdims=True)
    total = base + jnp.sum(ohf, axis=0, keepdims=True)
    carry[...] = total
    cnt_ref[...] = jnp.broadcast_to(total, cnt_ref.shape)

    id_cols = jnp.where(lane_i == 0, g_idx, jnp.where(lane_i == 1, rank, 0.0))
    ids_ref[...] = id_cols.T[:ROUTER_ID_ROWS, :].astype(jnp.int32)
    meta_ref[...] = jnp.where(lane_i == META_E1, i1 - lo,
                    jnp.where(lane_i == META_E2, i2 - lo,
                    jnp.where(lane_i == META_G1, gate1,
                    jnp.where(lane_i == META_G2, gate2, 0.0))))


def _router(x, w_pad, b_pad):
    T = x.shape[0]
    row = lambda i: (i, 0)
    fixed = lambda i: (0, 0)
    return pl.pallas_call(
        _router_kernel,
        grid=(T // ROW_TILE,),
        in_specs=[pl.BlockSpec((ROW_TILE, D_MODEL), row),
                  pl.BlockSpec((D_MODEL, LANES), fixed),
                  pl.BlockSpec((1, LANES), fixed)],
        out_specs=[pl.BlockSpec((ROUTER_ID_ROWS, ROW_TILE), lambda i: (0, i)),
                   pl.BlockSpec((ROW_TILE, LANES), row),
                   pl.BlockSpec((8, LANES), fixed)],
        out_shape=[jax.ShapeDtypeStruct((ROUTER_ID_ROWS, T), jnp.int32),
                   jax.ShapeDtypeStruct((T, LANES), F32),
                   jax.ShapeDtypeStruct((8, LANES), F32)],
        scratch_shapes=[pltpu.VMEM((1, LANES), F32)],
        compiler_params=_params(("arbitrary",)),
        name="moe_router",
    )(x, w_pad, b_pad)


def _tile_row(ref, row, sublanes):
    return ref.at[pl.ds(pl.multiple_of(row * sublanes, sublanes), sublanes)]


def _tile_row_copy(src_ref, src_row, dst_ref, dst_row, sem, sublanes):
    return pltpu.make_async_copy(_tile_row(src_ref, src_row, sublanes),
                                 _tile_row(dst_ref, dst_row, sublanes), sem)


def _tile_rows_wait(src_ref, dst_ref, n_rows, sem, sublanes):
    n = n_rows * sublanes
    pltpu.make_async_copy(src_ref.at[pl.ds(0, n)], dst_ref.at[pl.ds(0, n)], sem).wait()


def _load_tile_rows(ref, n_rows, sublanes):
    return jnp.concatenate([ref[pl.ds(c, n_rows, stride=sublanes), :]
                            for c in range(ROW_SUBLANES)], axis=1)


def _store_tile_rows(ref, val, sublanes):
    for c in range(ROW_SUBLANES):
        ref[pl.ds(c, val.shape[0], stride=sublanes), :] = val[:, c * LANES:(c + 1) * LANES]


def _dispatch_kernel(dest_ref, zero_ref, x_ref, meta_ref, buf_hbm, row_scr, sem):
    tm = x_ref.shape[0]
    ts = TOKEN_SUBLANES
    i = pl.program_id(0)
    n = pl.num_programs(0)
    slot = i % 2

    @pl.when(i == 0)
    def _():
        row_scr[...] = jnp.zeros_like(row_scr)

        def zero_block(b, start):
            @pl.when(zero_ref[b] != 0)
            def _():
                for k in range(MOE_ROWS // tm):
                    rows = pl.ds((b * MOE_ROWS + k * tm) * ts, tm * ts)
                    cp = pltpu.make_async_copy(row_scr.at[1], buf_hbm.at[rows], sem.at[1])
                    if start:
                        cp.start()
                    else:
                        cp.wait()
        for b in range(zero_ref.shape[0]):
            zero_block(b, True)
        for b in range(zero_ref.shape[0]):
            zero_block(b, False)

    stage = row_scr.at[slot]

    @pl.when(i >= 2)
    def _():
        _tile_rows_wait(stage, buf_hbm, tm, sem.at[slot], ts)

    _store_tile_rows(stage, x_ref[...], ts)
    stage[pl.ds(ROW_SUBLANES, tm, stride=ts), :] = meta_ref[...]

    def issue(pair, _):
        for k in range(DMA_PRIORITIES):
            r = pair * DMA_PRIORITIES + k
            _tile_row_copy(stage, r, buf_hbm, dest_ref[i * tm + r], sem.at[slot],
                           ts).start(priority=k)
        return 0
    lax.fori_loop(0, tm // DMA_PRIORITIES, issue, 0, unroll=ISSUE_UNROLL // DMA_PRIORITIES)

    @pl.when(i == n - 1)
    def _():
        _tile_rows_wait(stage, buf_hbm, tm, sem.at[slot], ts)

        @pl.when(n >= 2)
        def _():
            _tile_rows_wait(row_scr.at[1 - slot], buf_hbm, tm, sem.at[1 - slot], ts)


def _dispatch(dest, zero_flag, x, meta, n_rows):
    T = x.shape[0]
    return pl.pallas_call(
        _dispatch_kernel,
        grid_spec=pltpu.PrefetchScalarGridSpec(
            num_scalar_prefetch=2,
            grid=(T // COMBINE_TILE,),
            in_specs=[pl.BlockSpec((COMBINE_TILE, D_MODEL), lambda i, d, z: (i, 0)),
                      pl.BlockSpec((COMBINE_TILE, LANES), lambda i, d, z: (i, 0))],
            out_specs=pl.BlockSpec(memory_space=pl.ANY),
            scratch_shapes=[pltpu.VMEM((2, COMBINE_TILE * TOKEN_SUBLANES, LANES), F32),
                            pltpu.SemaphoreType.DMA((2,))]),
        out_shape=jax.ShapeDtypeStruct((n_rows * TOKEN_SUBLANES, LANES), F32),
        compiler_params=pltpu.CompilerParams(dimension_semantics=("arbitrary",),
                                             has_side_effects=True),
        name="moe_dispatch",
    )(dest, zero_flag, x, meta)

def _ffn_kernel(bg_ref, nused_ref, nxt_ref, nvalid_ref, rows_ref, w1_hbm, w3_hbm, w2_hbm, o_ref,
                w1_f32, w3_f32, w2_f32, w1_ref, w3_ref, w2_ref, sem, *, layer):
    i = pl.program_id(0)
    used = i < nused_ref[0]
    epg = MOE_EXPERTS_PER_GROUP

    def fetch(group):
        first = pl.multiple_of(group * epg, epg)
        return [pltpu.make_async_copy(src.at[layer, pl.ds(first, epg)], dst, sem.at[k])
                for k, (src, dst) in enumerate(((w1_hbm, w1_f32), (w3_hbm, w3_f32),
                                                (w2_hbm, w2_f32)))]

    @pl.when(i == 0)
    def _():
        for cp in fetch(bg_ref[0]):
            cp.start()

    @pl.when(used & ((i == 0) | (bg_ref[i] != bg_ref[jnp.maximum(i - 1, 0)])))
    def _():
        for cp in fetch(bg_ref[i]):
            cp.wait()
        for e in range(epg):
            w1_ref[e] = w1_f32[e].astype(BF16)
            w3_ref[e] = w3_f32[e].astype(BF16)
            w2_ref[e] = w2_f32[e].astype(BF16)

        @pl.when(nxt_ref[i] >= 0)
        def _():
            for cp in fetch(nxt_ref[i]):
                cp.start()

    def compute(n_rows):
        x = _load_tile_rows(rows_ref, n_rows, TOKEN_SUBLANES).astype(BF16)
        meta = rows_ref[pl.ds(ROW_SUBLANES, n_rows, stride=TOKEN_SUBLANES), :]
        e1 = meta[:, META_E1:META_E1 + 1]
        e2 = meta[:, META_E2:META_E2 + 1]
        g1 = meta[:, META_G1:META_G1 + 1]
        g2 = meta[:, META_G2:META_G2 + 1]
        acc = None
        for e in range(epg):
            h1 = _dot(x, w1_ref[e])
            h3 = _dot(x, w3_ref[e])
            gate = jnp.where(e1 == e, g1, g2)
            h = h1 * (1.0 / (1.0 + jnp.exp(-h1))) * h3 * gate
            h = jnp.where((e1 == e) | (e2 == e), h, 0.0).astype(BF16)
            y = _dot(h, w2_ref[e])
            acc = y if acc is None else acc + y
        _store_tile_rows(o_ref, acc, ROW_SUBLANES)
        if n_rows < MOE_ROWS:
            rest = (MOE_ROWS - n_rows) * ROW_SUBLANES
            o_ref[pl.ds(n_rows * ROW_SUBLANES, rest), :] = jnp.zeros((rest, LANES), F32)

    half_full = nvalid_ref[i] <= MOE_ROWS // 2

    @pl.when(used & jnp.logical_not(half_full))
    def _():
        compute(MOE_ROWS)

    @pl.when(used & half_full)
    def _():
        compute(MOE_ROWS // 2)

    @pl.when(jnp.logical_not(used))
    def _():
        o_ref[...] = jnp.zeros_like(o_ref)


def _expert_ffn(blk_group, n_used, next_group, n_valid, buf, w1, w3, w2, layer):
    n_rows = buf.shape[0] // TOKEN_SUBLANES
    epg = MOE_EXPERTS_PER_GROUP
    up = (epg, D_MODEL, MOE_D_EXPERT)
    down = (epg, MOE_D_EXPERT, D_MODEL)
    return pl.pallas_call(
        functools.partial(_ffn_kernel, layer=layer),
        grid_spec=pltpu.PrefetchScalarGridSpec(
            num_scalar_prefetch=4,
            grid=(n_rows // MOE_ROWS,),
            in_specs=[pl.BlockSpec((MOE_ROWS * TOKEN_SUBLANES, LANES),
                                   lambda i, bg, nu, nx, nv: (jnp.minimum(i, nu[0] - 1), 0)),
                      pl.BlockSpec(memory_space=pl.ANY),
                      pl.BlockSpec(memory_space=pl.ANY),
                      pl.BlockSpec(memory_space=pl.ANY)],
            out_specs=pl.BlockSpec((MOE_ROWS * ROW_SUBLANES, LANES),
                                   lambda i, bg, nu, nx, nv: (i, 0)),
            scratch_shapes=[pltpu.VMEM(up, F32), pltpu.VMEM(up, F32), pltpu.VMEM(down, F32),
                            pltpu.VMEM(up, BF16), pltpu.VMEM(up, BF16), pltpu.VMEM(down, BF16),
                            pltpu.SemaphoreType.DMA((3,))]),
        out_shape=jax.ShapeDtypeStruct((n_rows * ROW_SUBLANES, LANES), F32),
        compiler_params=_params(("arbitrary",), vmem=FFN_VMEM_LIMIT),
        name="moe_ffn",
    )(blk_group, n_used, next_group, n_valid, buf, w1, w3, w2)


def _combine_kernel(dest_ref, y_hbm, x_ref, g_ref, b_ref, o_ref, y_scr, sem):
    tm = x_ref.shape[0]
    i = pl.program_id(0)

    def gather(tile, slot):
        for r in range(tm):
            _tile_row_copy(y_hbm, dest_ref[tile * tm + r], y_scr.at[slot], r, sem.at[slot],
                           ROW_SUBLANES).start(priority=r % DMA_PRIORITIES)

    @pl.when(i == 0)
    def _():
        gather(0, 0)

    @pl.when(i + 1 < pl.num_programs(0))
    def _():
        gather(i + 1, (i + 1) % 2)

    slot = i % 2
    _tile_rows_wait(y_hbm, y_scr.at[slot], tm, sem.at[slot], ROW_SUBLANES)
    y = DEEPNORM_ALPHA * x_ref[...] + _load_tile_rows(y_scr.at[slot], tm, ROW_SUBLANES)
    o_ref[...] = _layer_norm(y, g_ref[...], b_ref[...])


def _combine_ln(dest, y_buf, x, g, b):
    T = x.shape[0]
    row = lambda i, d: (i, 0)
    fixed = lambda i, d: (0, 0)
    return pl.pallas_call(
        _combine_kernel,
        grid_spec=pltpu.PrefetchScalarGridSpec(
            num_scalar_prefetch=1,
            grid=(T // COMBINE_TILE,),
            in_specs=[pl.BlockSpec(memory_space=pl.ANY),
                      pl.BlockSpec((COMBINE_TILE, D_MODEL), row),
                      pl.BlockSpec((1, D_MODEL), fixed),
                      pl.BlockSpec((1, D_MODEL), fixed)],
            out_specs=pl.BlockSpec((COMBINE_TILE, D_MODEL), row),
            scratch_shapes=[pltpu.VMEM((2, COMBINE_TILE * ROW_SUBLANES, LANES), F32),
                            pltpu.SemaphoreType.DMA((2,))]),
        out_shape=jax.ShapeDtypeStruct((T, D_MODEL), F32),
        compiler_params=_params(("arbitrary",)),
        name="moe_combine_ln",
    )(dest, y_buf, x, g.reshape(1, -1), b.reshape(1, -1))


def _hier_moe_ln(x, w_grp, b_grp, w_rt, b_rt, w1, w3, w2, layer, g2, b2):
    T = x.shape[0]
    pad_w = LANES - MOE_GROUPS - MOE_EXPERTS
    w_pad = jnp.concatenate([w_grp, w_rt, jnp.zeros((D_MODEL, pad_w), F32)], axis=1)
    b_pad = jnp.concatenate([b_grp, b_rt, jnp.zeros((pad_w,), F32)]).reshape(1, LANES)
    ids, meta, cnt = _router(x, w_pad, b_pad)

    counts = cnt[0, ROUTER_GROUP_LANE0:ROUTER_GROUP_LANE0 + MOE_GROUPS].astype(jnp.int32)
    padded = ((counts + MOE_ROWS - 1) // MOE_ROWS) * MOE_ROWS
    pad_end = jnp.cumsum(padded)
    pad_start = pad_end - padded
    dest = ids[1]
    for g in range(MOE_GROUPS):
        dest = dest + jnp.where(ids[0] == g, pad_start[g], 0)
    n_blk = T // MOE_ROWS + MOE_GROUPS
    n_used = (pad_end[-1:] // MOE_ROWS).astype(jnp.int32)
    blk_row0 = jnp.minimum(jnp.arange(n_blk), n_used - 1) * MOE_ROWS
    blk_group = jnp.sum(pad_end[None, :] <= blk_row0[:, None], axis=1).astype(jnp.int32)
    blk_end = (jnp.arange(n_blk) + 1) * MOE_ROWS
    group_last = jnp.any((blk_end[:, None] == pad_end[None, :]) & (padded[None, :] > 0), axis=1)
    zero_flag = (group_last | (jnp.arange(n_blk) >= n_used)).astype(jnp.int32)

    after = pad_end[blk_group] // MOE_ROWS
    next_group = jnp.where(after < n_used, blk_group[jnp.minimum(after, n_blk - 1)], -1)
    next_group = next_group.astype(jnp.int32)

    buf = _dispatch(dest, zero_flag, x, meta, n_blk * MOE_ROWS)
    n_valid = jnp.clip((pad_start + counts)[blk_group] - jnp.arange(n_blk) * MOE_ROWS,
                       0, MOE_ROWS).astype(jnp.int32)
    y_buf = _expert_ffn(blk_group, n_used, next_group, n_valid, buf, w1, w3, w2, layer)
    return _combine_ln(dest, y_buf, x, g2, b2)


def kernel(x, moba_w_qkv, moba_w_o, gmlp_w_in, gmlp_b_in, gmlp_ln_g, gmlp_ln_b, gmlp_w_s,
           gmlp_b_s, gmlp_w_out, ln1_g, ln1_b, ln2_g, ln2_b, moe_w_grp, moe_b_grp, moe_w_rt,
           moe_b_rt, moe_w1, moe_w3, moe_w2):
    B, S, D = x.shape
    assert D == D_MODEL and S % MOBA_BLOCK == 0 and (B * S) % ROW_TILE == 0
    xf = x.reshape(B * S, D)
    for i in range(DEPTH):
        j = i // 2
        if i % 2 == 0:
            wk = moba_w_qkv[j, :, D:2 * D].astype(BF16)
            wqvt = jnp.concatenate([moba_w_qkv[j, :, :D], moba_w_qkv[j, :, 2 * D:]],
                                   axis=1).T.astype(BF16)
            k, qvt = _qkv_proj(xf, wk, wqvt)
            att = _moba_attention(k, qvt, B, S)
            xf = _proj_ln(att, moba_w_o[j].astype(BF16), xf, ln1_g[i], ln1_b[i])
        else:
            xf = _gmlp_layer(xf, gmlp_w_in[j].astype(BF16), gmlp_b_in[j], gmlp_ln_g[j],
                             gmlp_ln_b[j], gmlp_w_s[j], gmlp_b_s[j].T,
                             gmlp_w_out[j].astype(BF16), ln1_g[i], ln1_b[i])
        xf = _hier_moe_ln(xf, moe_w_grp[i], moe_b_grp[i], moe_w_rt[i], moe_b_rt[i],
                          moe_w1, moe_w3, moe_w2, i, ln2_g[i], ln2_b[i])
    return xf.reshape(B, S, D)
```
